```python
import math
import jax
import jax.numpy as jnp
from jax import lax
import numpy as np

D_MODEL = 1024
BATCH = 8
SEQ = 2048
DEPTH = 4
DEC_BATCH = 128
DEC_SEQ = 8
PAST_LEN = 8192
PAGE_SIZE = 128

N_META = 16
Q_BLOCK = 128
EPS = 1e-6
NEG_INF = -1e30

A_HEADS = 4
A_KV_HEADS = 2
A_HEAD_DIM = 64
FORGET_BIAS = 2.0
B_HEADS = 4
B_NOPE = 64
B_ROPE = 32
B_V = 64
B_Q_LORA = 256
B_KV_LORA = 128
B_THETA = 10000.0
C_HEADS = 4
C_KV_HEADS = 1
C_HEAD_DIM = 64
C_V = 2 * C_HEAD_DIM
C_ROT = C_HEAD_DIM // 4
ROPE_THETA = 500000.0

D_FF = 4 * D_MODEL
A_OUT = A_HEADS * A_HEAD_DIM
B_OUT = B_HEADS * B_V
C_OUT = C_HEADS * C_V
MIX_W = A_OUT + B_OUT + C_OUT
B_LAT = B_KV_LORA + B_ROPE
IN_WIDTHS = (A_HEADS * A_HEAD_DIM, A_KV_HEADS * A_HEAD_DIM, A_KV_HEADS * A_HEAD_DIM, A_HEADS,
             B_Q_LORA, B_KV_LORA, B_ROPE,
             C_HEADS * 2 * C_HEAD_DIM, C_KV_HEADS * 2 * C_HEAD_DIM, C_KV_HEADS * C_V)
N_IN = sum(IN_WIDTHS)
A_SCALE = A_HEAD_DIM ** -0.5
B_SCALE = (B_NOPE + B_ROPE) ** -0.5
C_SCALE = C_HEAD_DIM ** -0.5

kernel_name = 'hybrid_fox_mla_diffattn_decoder_step'


def rmsnorm(x, g):
    xf = x.astype(jnp.float32)
    y = xf * lax.rsqrt(jnp.mean(xf * xf, axis=-1, keepdims=True) + EPS)
    return (y * g.astype(jnp.float32)).astype(x.dtype)


def rotary(x, pos, theta, rot_dim):
    half = rot_dim // 2
    inv_freq = theta ** (-2.0 * jnp.arange(half, dtype=jnp.float32) / rot_dim)
    ang = pos[:, None] * inv_freq[None, :]
    cos = jnp.cos(ang)[:, None, :]
    sin = jnp.sin(ang)[:, None, :]
    xf = x.astype(jnp.float32)
    x1 = xf[..., :half]
    x2 = xf[..., half:rot_dim]
    out = jnp.concatenate([x1 * cos - x2 * sin, x1 * sin + x2 * cos, xf[..., rot_dim:]], axis=-1)
    return out.astype(x.dtype)


def masked_softmax(s, mask):
    return jax.nn.softmax(jnp.where(mask, s, NEG_INF), axis=-1)


def mixer_inputs(h, pos, p):
    bsz, t, _ = h.shape
    z = h @ p['w_in']
    offs = np.cumsum(IN_WIDTHS)[:-1].tolist()
    za_q, za_k, za_v, za_f, zb_cq, zb_ckv, zb_kr, zc_q, zc_k, zc_v = jnp.split(z, offs, axis=-1)
    qa = za_q.reshape(bsz, t, A_HEADS, A_HEAD_DIM)
    a_kv = jnp.stack([za_k.reshape(bsz, t, A_KV_HEADS, A_HEAD_DIM),
                      za_v.reshape(bsz, t, A_KV_HEADS, A_HEAD_DIM)], axis=2)
    logf = jax.nn.log_sigmoid(za_f.astype(jnp.float32) + p['b_f'].astype(jnp.float32))
    cq = rmsnorm(zb_cq, p['g_cq'])
    qb = (cq @ p['w_uq']).reshape(bsz, t, B_HEADS, B_NOPE + B_ROPE)
    q_lat = jnp.einsum('bthn,chn->bthc', qb[..., :B_NOPE], p['w_uk'])
    qb_cat = jnp.concatenate([q_lat, rotary(qb[..., B_NOPE:], pos, B_THETA, B_ROPE)], axis=-1)
    k_rope = rotary(zb_kr[:, :, None, :], pos, B_THETA, B_ROPE)[:, :, 0]
    b_lat = jnp.concatenate([rmsnorm(zb_ckv, p['g_ckv']), k_rope], axis=-1)
    qc = rotary(zc_q.reshape(bsz, t, C_HEADS * 2, C_HEAD_DIM), pos, ROPE_THETA, C_ROT)
    qc = qc.reshape(bsz, t, C_HEADS, 2, C_HEAD_DIM)
    kc = rotary(zc_k.reshape(bsz, t, C_KV_HEADS * 2, C_HEAD_DIM), pos, ROPE_THETA, C_ROT)
    kc = kc.reshape(bsz, t, C_KV_HEADS, 2 * C_HEAD_DIM)
    c_kv = jnp.stack([kc, zc_v.reshape(bsz, t, C_KV_HEADS, C_V)], axis=2)
    return (qa, qb_cat, qc), (a_kv, logf, b_lat, c_kv)


def fox_attention(q, a_kv, bias, mask):
    bsz, tq = q.shape[:2]
    tk = a_kv.shape[1]
    r = A_HEADS // A_KV_HEADS
    k = a_kv[:, :, 0]
    v = a_kv[:, :, 1]
    s = jnp.einsum('bqgrd,bkgd->bgrqk', q.reshape(bsz, tq, A_KV_HEADS, r, A_HEAD_DIM), k,
                   preferred_element_type=jnp.float32) * A_SCALE
    s = s + bias.reshape(bsz, A_KV_HEADS, r, tq, tk)
    pr = masked_softmax(s, mask)
    o = jnp.einsum('bgrqk,bkgd->bqgrd', pr.astype(v.dtype), v)
    return o.reshape(bsz, tq, A_OUT)


def mla_attention(q_cat, b_lat, mask, w_uv):
    bsz, tq = q_cat.shape[:2]
    s = jnp.einsum('bqhc,bkc->bhqk', q_cat, b_lat, preferred_element_type=jnp.float32) * B_SCALE
    pr = masked_softmax(s, mask)
    o_lat = jnp.einsum('bhqk,bkc->bqhc', pr.astype(b_lat.dtype), b_lat[..., :B_KV_LORA])
    o = jnp.einsum('bqhc,chv->bqhv', o_lat, w_uv)
    return o.reshape(bsz, tq, B_OUT)


def diff_attention(q, c_kv, mask, lam, lam_init, g_subln):
    bsz, tq = q.shape[:2]
    tk = c_kv.shape[1]
    r = C_HEADS // C_KV_HEADS
    k = c_kv[:, :, 0].reshape(bsz, tk, C_KV_HEADS, 2, C_HEAD_DIM)
    v = c_kv[:, :, 1]
    s = jnp.einsum('bqgrjd,bkgjd->bgrjqk', q.reshape(bsz, tq, C_KV_HEADS, r, 2, C_HEAD_DIM), k,
                   preferred_element_type=jnp.float32) * C_SCALE
    pr = masked_softmax(s, mask)
    a = pr[:, :, :, 0] - lam * pr[:, :, :, 1]
    o = jnp.einsum('bgrqk,bkgv->bqgrv', a.astype(v.dtype), v).reshape(bsz, tq, C_HEADS, C_V)
    o = rmsnorm(o, g_subln) * (1.0 - lam_init)
    return o.reshape(bsz, tq, C_OUT)


def mix_heads(qa, qb, qc, a_kv, b_lat, c_kv, bias, mask, p, lam, lam_init):
    return jnp.concatenate([fox_attention(qa, a_kv, bias, mask),
                            mla_attention(qb, b_lat, mask, p['w_uv']),
                            diff_attention(qc, c_kv, mask, lam, lam_init, p['g_subln'])], axis=-1)


def finish_layer(x, mix, p):
    x = x + rmsnorm(mix @ p['w_out'], p['g_post_mix'])
    h = rmsnorm(x, p['g_pre_mlp'])
    m = jnp.square(jax.nn.relu(h @ p['w_up'])) @ p['w_down']
    return x + rmsnorm(m, p['g_post_mlp'])


def setup_inputs(seed: int = 0) -> dict:
    key = jax.random.key(seed)
    ks = jax.random.split(key, 32)
    f32 = jnp.float32
    n_pages = PAST_LEN // PAGE_SIZE
    n_phys = (DEC_BATCH * n_pages * 5) // 4

    def nrm(k, shape, scale):
        return jax.random.normal(k, shape, f32) * scale

    def gain(k, shape):
        return 1.0 + 0.05 * jax.random.normal(k, shape, f32)

    page_table = jax.random.permutation(ks[0], n_phys)[: DEC_BATCH * n_pages]
    page_table = page_table.reshape(DEC_BATCH, n_pages).astype(jnp.int32)
    return {
        'x_prompt': nrm(ks[1], (BATCH, SEQ, D_MODEL), 1.0),
        'x_sample': nrm(ks[2], (DEC_BATCH, DEC_SEQ, D_MODEL), 1.0),
        'cache_a_kv': nrm(ks[3], (DEPTH, n_phys, PAGE_SIZE, 2, A_KV_HEADS, A_HEAD_DIM), 1.0),
        'cache_a_logf': jax.nn.log_sigmoid(FORGET_BIAS + jax.random.normal(ks[4], (DEPTH, n_phys, PAGE_SIZE, A_HEADS), f32)),
        'cache_b_lat': nrm(ks[5], (DEPTH, n_phys, PAGE_SIZE, B_LAT), 1.0),
        'cache_c_kv': nrm(ks[6], (DEPTH, n_phys, PAGE_SIZE, 2, C_KV_HEADS, C_V), 1.0),
        'page_table': page_table,
        'meta_tokens': nrm(ks[7], (N_META, D_MODEL), 1.0),
        'w_in': nrm(ks[8], (DEPTH, D_MODEL, N_IN), D_MODEL ** -0.5),
        'b_f': FORGET_BIAS + nrm(ks[9], (DEPTH, A_HEADS), 0.1),
        'g_cq': gain(ks[10], (DEPTH, B_Q_LORA)),
        'g_ckv': gain(ks[11], (DEPTH, B_KV_LORA)),
        'w_uq': nrm(ks[12], (DEPTH, B_Q_LORA, B_HEADS * (B_NOPE + B_ROPE)), B_Q_LORA ** -0.5),
        'w_uk': nrm(ks[13], (DEPTH, B_KV_LORA, B_HEADS, B_NOPE), B_KV_LORA ** -0.5),
        'w_uv': nrm(ks[14], (DEPTH, B_KV_LORA, B_HEADS, B_V), B_KV_LORA ** -0.5),
        'lam_q1': nrm(ks[15], (DEPTH, C_HEAD_DIM), 0.1),
        'lam_k1': nrm(ks[16], (DEPTH, C_HEAD_DIM), 0.1),
        'lam_q2': nrm(ks[17], (DEPTH, C_HEAD_DIM), 0.1),
        'lam_k2': nrm(ks[18], (DEPTH, C_HEAD_DIM), 0.1),
        'g_subln': gain(ks[19], (DEPTH, C_V)),
        'w_out': nrm(ks[20], (DEPTH, MIX_W, D_MODEL), MIX_W ** -0.5),
        'g_pre_mix': gain(ks[21], (DEPTH, D_MODEL)),
        'g_post_mix': gain(ks[22], (DEPTH, D_MODEL)),
        'g_pre_mlp': gain(ks[23], (DEPTH, D_MODEL)),
        'g_post_mlp': gain(ks[24], (DEPTH, D_MODEL)),
        'w_up': nrm(ks[25], (DEPTH, D_MODEL, D_FF), D_MODEL ** -0.5),
        'w_down': nrm(ks[26], (DEPTH, D_FF, D_MODEL), D_FF ** -0.5),
    }


def reference(x_prompt, x_sample, cache_a_kv, cache_a_logf, cache_b_lat, cache_c_kv, page_table,
              meta_tokens, w_in, b_f, g_cq, g_ckv, w_uq, w_uk, w_uv, lam_q1, lam_k1, lam_q2, lam_k2,
              g_subln, w_out, g_pre_mix, g_post_mix, g_pre_mlp, g_post_mlp, w_up, w_down):
    f32 = jnp.float32
    bsz, seq = x_prompt.shape[:2]
    L = N_META + seq
    Lp = -(-L // Q_BLOCK) * Q_BLOCK
    n_blocks = Lp // Q_BLOCK
    meta = jnp.broadcast_to(meta_tokens[None].astype(x_prompt.dtype), (bsz, N_META, D_MODEL))
    xp = jnp.concatenate([meta, x_prompt, jnp.zeros((bsz, Lp - L, D_MODEL), x_prompt.dtype)], axis=1)
    xs = x_sample
    db, tn = xs.shape[:2]
    tp = page_table.shape[1] * PAGE_SIZE
    pos_p = jnp.arange(Lp, dtype=f32)
    pos_s = jnp.arange(tn, dtype=f32) + PAST_LEN
    kpos = jnp.arange(Lp)
    mask_s = jnp.concatenate([jnp.ones((tn, tp), bool), jnp.tril(jnp.ones((tn, tn), bool))], axis=1)

    akv_p, alf_p, blat_p, ckv_p = [], [], [], []
    akv_s, alf_s, blat_s, ckv_s = [], [], [], []
    for l in range(DEPTH):
        p = {'w_in': w_in[l], 'b_f': b_f[l], 'g_cq': g_cq[l], 'g_ckv': g_ckv[l], 'w_uq': w_uq[l],
             'w_uk': w_uk[l], 'w_uv': w_uv[l], 'g_subln': g_subln[l], 'w_out': w_out[l],
             'g_post_mix': g_post_mix[l], 'g_pre_mlp': g_pre_mlp[l], 'g_post_mlp': g_post_mlp[l],
             'w_up': w_up[l], 'w_down': w_down[l]}
        lam_init = 0.8 - 0.6 * math.exp(-0.3 * l)
        lam = (jnp.exp(jnp.sum(lam_q1[l].astype(f32) * lam_k1[l].astype(f32)))
               - jnp.exp(jnp.sum(lam_q2[l].astype(f32) * lam_k2[l].astype(f32))) + lam_init)

        (qa, qb, qc), (a_kv, logf, b_lat, c_kv) = mixer_inputs(rmsnorm(xp, g_pre_mix[l]), pos_p, p)
        cum = jnp.swapaxes(jnp.cumsum(logf, axis=1), 1, 2)

        def block(i):
            q0 = i * Q_BLOCK
            sl = lambda a: lax.dynamic_slice_in_dim(a, q0, Q_BLOCK, axis=1)
            qpos = q0 + jnp.arange(Q_BLOCK)
            mask = kpos[None, :] <= qpos[:, None]
            cq_blk = lax.dynamic_slice_in_dim(cum, q0, Q_BLOCK, axis=2)
            bias = cq_blk[..., :, None] - cum[..., None, :]
            return mix_heads(sl(qa), sl(qb), sl(qc), a_kv, b_lat, c_kv, bias, mask, p, lam, lam_init)

        outs = lax.map(block, jnp.arange(n_blocks))
        mix_p = jnp.swapaxes(outs, 0, 1).reshape(bsz, Lp, MIX_W)
        xp = finish_layer(xp, mix_p, p)
        akv_p.append(a_kv[:, :L])
        alf_p.append(logf[:, :L])
        blat_p.append(b_lat[:, :L])
        ckv_p.append(c_kv[:, :L])

        (qa, qb, qc), (a_kv_n, logf_n, b_lat_n, c_kv_n) = mixer_inputs(rmsnorm(xs, g_pre_mix[l]), pos_s, p)
        a_kv_all = jnp.concatenate([cache_a_kv[l, page_table].reshape(db, tp, 2, A_KV_HEADS, A_HEAD_DIM), a_kv_n], axis=1)
        b_lat_all = jnp.concatenate([cache_b_lat[l, page_table].reshape(db, tp, B_LAT), b_lat_n], axis=1)
        c_kv_all = jnp.concatenate([cache_c_kv[l, page_table].reshape(db, tp, 2, C_KV_HEADS, C_V), c_kv_n], axis=1)
        logf_past = jnp.swapaxes(cache_a_logf[l, page_table].reshape(db, tp, A_HEADS).astype(f32), 1, 2)
        d_past = lax.cumsum(logf_past, axis=2, reverse=True) - logf_past
        c_new = jnp.cumsum(jnp.swapaxes(logf_n, 1, 2), axis=2)
        bias_s = jnp.concatenate([c_new[..., :, None] + d_past[..., None, :],
                                  c_new[..., :, None] - c_new[..., None, :]], axis=-1)
        mix_s = mix_heads(qa, qb, qc, a_kv_all, b_lat_all, c_kv_all, bias_s, mask_s, p, lam, lam_init)
        xs = finish_layer(xs, mix_s, p)
        akv_s.append(a_kv_n)
        alf_s.append(logf_n)
        blat_s.append(b_lat_n)
        ckv_s.append(c_kv_n)

    y_prompt = xp[:, N_META:L]
    y_sample = xs
    return (y_prompt, y_sample,
            jnp.stack(akv_p), jnp.stack(alf_p), jnp.stack(blat_p), jnp.stack(ckv_p),
            jnp.stack(akv_s), jnp.stack(alf_s), jnp.stack(blat_s), jnp.stack(ckv_s))
```

```python
import functools
import math

import numpy as np
import jax
import jax.numpy as jnp
from jax import lax
from jax.experimental import pallas as pl
from jax.experimental.pallas import tpu as pltpu

F32 = jnp.float32
BF16 = jnp.bfloat16

LANES = 128
SUBLANES = 8
VMEM_LIMIT_BYTES = 56 * 1024 * 1024

N_META = 16
Q_BLOCK = 128
KV_BLOCK = 2 * Q_BLOCK
PAGE = 128
EPS = 1e-6
NEG_INF = -1e30

A_HEADS, A_KV_HEADS, A_DIM = 4, 2, 64
B_HEADS, B_NOPE, B_ROPE, B_V, B_Q_LORA, B_KV_LORA = 4, 64, 32, 64, 256, 128
B_THETA = 10000.0
C_HEADS, C_DIM = 4, 64
C_V = 2 * C_DIM
C_ROT = C_DIM // 4
C_THETA = 500000.0
A_SCALE = A_DIM ** -0.5
B_SCALE = (B_NOPE + B_ROPE) ** -0.5
C_SCALE = C_DIM ** -0.5

LOGF_LANE = 32
N_Z = 14 * LANES
MIX_W = 256 + 512 + 512
PAGES_PER_CHUNK = 8


def _dot(a, b):
    return jnp.dot(a, b, preferred_element_type=F32)


def _dot_nt(a, b):
    return lax.dot_general(a, b, (((1,), (1,)), ((), ())), preferred_element_type=F32)


def _split3(x):
    hi = x.astype(BF16)
    r = x - hi.astype(F32)
    mid = r.astype(BF16)
    lo = (r - mid.astype(F32)).astype(BF16)
    return hi, mid, lo


def _rms(x, g):
    return x * lax.rsqrt(jnp.mean(x * x, axis=-1, keepdims=True) + EPS) * g


def _rot(x, cos, s_next, s_prev, half):
    n = x.shape[-1]
    return x * cos + pltpu.roll(x, n - half, 1) * s_next + pltpu.roll(x, half, 1) * s_prev


def _lane(shape):
    return lax.broadcasted_iota(jnp.int32, shape, len(shape) - 1)


def _lam(lam_ref):
    t1 = jnp.sum(lam_ref[0:1, :] * lam_ref[1:2, :], axis=-1, keepdims=True)
    t2 = jnp.sum(lam_ref[2:3, :] * lam_ref[3:4, :], axis=-1, keepdims=True)
    lam_init = lam_ref[4:5, 0:1]
    return jnp.exp(t1) - jnp.exp(t2) + lam_init, lam_init


def _in_proj_kernel(sample, x_ref, g_ref, w_ref, tab_ref, bf_ref, gcq_ref, wuq_ref, wuk_ref,
                    place_ref, gckv_ref, *refs):
    if sample:
        tri_ref, qa_ref, kva_ref, lf_ref, qb_ref, kb_ref, qc_ref, kvc_ref, cn_ref = refs
    else:
        (qa_ref, kva_ref, kva16_ref, lf_ref, qb_ref, kb_ref, kb16_ref, qc_ref, kvc_ref,
         kvc16_ref) = refs
    h = _rms(x_ref[...], g_ref[...]).astype(BF16)
    z = _dot(h, w_ref[...])

    def blk(i, n=1):
        return z[:, i * LANES:(i + n) * LANES]

    cos_b, sn_b, sp_b = tab_ref[0], tab_ref[1], tab_ref[2]
    cos_c, sn_c, sp_c = tab_ref[3], tab_ref[4], tab_ref[5]

    qa_ref[...] = (blk(0, 2) * A_SCALE).astype(qa_ref.dtype)
    kva = blk(2, 2)
    kva_ref[...] = kva
    last = blk(13)
    v = last + bf_ref[...]
    lf = jnp.minimum(v, 0.0) - jnp.log1p(jnp.exp(-jnp.abs(v)))
    lf_ref[...] = lf

    cq = _rms(blk(4, 2), gcq_ref[...]).astype(BF16)
    qb = _dot(cq, wuq_ref[...])
    q_lat = _dot(qb[:, 0:256].astype(BF16), wuk_ref[...]) * B_SCALE
    q_rope = _rot(qb[:, 256:384], cos_b, sn_b, sp_b, B_ROPE // 2) * B_SCALE
    q_rope = _dot(q_rope.astype(BF16), place_ref[...])
    for hh in range(B_HEADS):
        qb_ref[:, (2 * hh) * LANES:(2 * hh + 1) * LANES] = (
            q_lat[:, hh * LANES:(hh + 1) * LANES].astype(qb_ref.dtype))
        qb_ref[:, (2 * hh + 1) * LANES:(2 * hh + 2) * LANES] = (
            q_rope[:, hh * LANES:(hh + 1) * LANES].astype(qb_ref.dtype))
    ckv = _rms(blk(6), gckv_ref[...])
    k_rope = _rot(last, cos_b, sn_b, sp_b, B_ROPE // 2)
    k_rope = jnp.where(_lane(k_rope.shape) < B_ROPE, k_rope, 0.0)
    kb_ref[:, 0:LANES] = ckv
    kb_ref[:, LANES:2 * LANES] = k_rope

    for j in range(4):
        qc_ref[:, j * LANES:(j + 1) * LANES] = (
            _rot(blk(7 + j), cos_c, sn_c, sp_c, C_ROT // 2) * C_SCALE).astype(qc_ref.dtype)
    kc = _rot(blk(11), cos_c, sn_c, sp_c, C_ROT // 2)
    vc = blk(12)
    kvc_ref[:, 0:LANES] = kc
    kvc_ref[:, LANES:2 * LANES] = vc

    if sample:
        hi, mid, lo = _split3(lf)
        tri = tri_ref[...]
        cn_ref[...] = _dot(tri, hi) + _dot(tri, mid) + _dot(tri, lo)
    else:
        kva16_ref[...] = kva.astype(BF16)
        kb16_ref[:, 0:LANES] = ckv.astype(BF16)
        kb16_ref[:, LANES:2 * LANES] = k_rope.astype(BF16)
        kvc16_ref[:, 0:LANES] = kc.astype(BF16)
        kvc16_ref[:, LANES:2 * LANES] = vc.astype(BF16)


def _in_proj(x, tabs, p, tm, tabs_period_blocks, sample, tri=None):
    rows = x.shape[0]
    grid = (rows // tm,)
    row = lambda w: pl.BlockSpec((tm, w), lambda i: (i, 0))
    full = lambda a: pl.BlockSpec(a.shape, lambda i: (0,) * a.ndim)
    tab_spec = pl.BlockSpec((6, tm, LANES), lambda i: (0, i % tabs_period_blocks, 0))
    ins = [x, p['g_pre_mix'], p['w_in'], tabs, p['b_f'], p['g_cq'], p['w_uq'], p['w_uk'],
           p['place'], p['g_ckv']]
    in_specs = [row(x.shape[1]), full(p['g_pre_mix']), full(p['w_in']), tab_spec, full(p['b_f']),
                full(p['g_cq']), full(p['w_uq']), full(p['w_uk']), full(p['place']),
                full(p['g_ckv'])]
    sds = lambda w, dt: jax.ShapeDtypeStruct((rows, w), dt)
    if sample:
        ins.append(tri)
        in_specs.append(full(tri))
        outs = [(256, F32), (256, F32), (LANES, F32), (1024, F32), (256, F32), (512, F32),
                (256, F32), (LANES, F32)]
    else:
        outs = [(256, BF16), (256, F32), (256, BF16), (LANES, F32), (1024, BF16), (256, F32),
                (256, BF16), (512, BF16), (256, F32), (256, BF16)]
    return pl.pallas_call(
        functools.partial(_in_proj_kernel, sample),
        grid=grid,
        in_specs=in_specs,
        out_specs=[row(w) for w, _ in outs],
        out_shape=[sds(w, dt) for w, dt in outs],
        compiler_params=pltpu.CompilerParams(
            dimension_semantics=("parallel",), vmem_limit_bytes=VMEM_LIMIT_BYTES),
        name="in_proj_sample" if sample else "in_proj_prompt",
    )(*ins)


def _fox_bias_kernel(lf_ref, upper_ref, ones_ref, out_ref):
    nb = out_ref.shape[1]
    rows = lax.broadcasted_iota(jnp.int32, (SUBLANES, LANES), 0)
    sel = (_lane((SUBLANES, LANES)) == rows + LOGF_LANE) & (rows < A_HEADS)
    sel = jnp.where(sel, 1.0, 0.0).astype(BF16)
    hi, mid, lo = _split3(lf_ref[...])
    lft = _dot_nt(sel, hi) + _dot_nt(sel, mid) + _dot_nt(sel, lo)
    x = jnp.concatenate([lft[:, j * LANES:(j + 1) * LANES] for j in range(nb)], axis=0)
    hi, mid, lo = _split3(x)
    up, ones = upper_ref[...], ones_ref[...]
    cum = _dot(hi, up) + _dot(mid, up) + _dot(lo, up)
    tot = _dot(hi, ones) + _dot(mid, ones) + _dot(lo, ones)
    carry = jnp.zeros((SUBLANES, LANES), F32)
    for j in range(nb):
        sl = slice(j * SUBLANES, (j + 1) * SUBLANES)
        out_ref[0, j] = -(cum[sl] + carry)
        carry = carry + tot[sl]


def _fox_bias(lf, bsz, lp):
    nb = lp // LANES
    upper = jnp.asarray(np.triu(np.ones((LANES, LANES), np.float32)), BF16)
    ones = jnp.ones((LANES, LANES), BF16)
    full = lambda a: pl.BlockSpec(a.shape, lambda b: (0,) * a.ndim)
    return pl.pallas_call(
        _fox_bias_kernel,
        grid=(bsz,),
        in_specs=[pl.BlockSpec((lp, LANES), lambda b: (b, 0)), full(upper), full(ones)],
        out_specs=pl.BlockSpec((1, nb, SUBLANES, LANES), lambda b: (b, 0, 0, 0)),
        out_shape=jax.ShapeDtypeStruct((bsz, nb, SUBLANES, LANES), F32),
        compiler_params=pltpu.CompilerParams(dimension_semantics=("parallel",)),
        name="fox_bias",
    )(lf, upper, ones)


def _softmax_pv(q, k_at, v_at, bias_at, qi, s_ref, mx_ref, acc_ref):
    r = q.shape[0]
    nfull = qi // 2
    tail_start = pl.multiple_of(jnp.maximum((qi - 1) * Q_BLOCK, 0), Q_BLOCK)
    tail_lo = nfull * KV_BLOCK
    mx_ref[0:r, :] = jnp.full((r, KV_BLOCK), NEG_INF, F32)

    def scores(start):
        s = _dot_nt(q, k_at(start))
        if bias_at is not None:
            s = s + bias_at(start)
        return s

    def full_block(j, carry):
        s = scores(pl.multiple_of(j * KV_BLOCK, KV_BLOCK))
        s_ref[j, 0:r, :] = s
        mx_ref[0:r, :] = jnp.maximum(mx_ref[0:r, :], s)
        return carry

    lax.fori_loop(0, nfull, full_block, 0)
    s = scores(tail_start)
    kpos = tail_start + _lane((r, KV_BLOCK))
    qpos = qi * Q_BLOCK + lax.broadcasted_iota(jnp.int32, (r, KV_BLOCK), 0) % Q_BLOCK
    s = jnp.where(kpos >= tail_lo, jnp.where(kpos <= qpos, s, NEG_INF), NEG_INF)
    s_ref[nfull, 0:r, :] = s
    m = jnp.max(jnp.maximum(mx_ref[0:r, :], s), axis=-1, keepdims=True)

    acc_ref[0:r, :] = jnp.zeros((r, 2 * LANES), F32)
    ones = jnp.ones((KV_BLOCK, LANES), BF16)

    def pv_block(j, carry):
        start = jnp.where(j < nfull, j * KV_BLOCK, tail_start)
        start = pl.multiple_of(start, Q_BLOCK)
        pr = jnp.exp(s_ref[j, 0:r, :] - m).astype(BF16)
        vext = jnp.concatenate([v_at(start), ones], axis=1)
        acc_ref[0:r, :] += _dot(pr, vext)
        return carry

    lax.fori_loop(0, nfull + 1, pv_block, 0)
    return acc_ref[0:r, 0:LANES] / acc_ref[0:r, LANES:2 * LANES]


def _prompt_attn_kernel(qa_ref, qb_ref, qc_ref, kva_ref, kb_ref, kvc_ref, kbias_ref, lam_ref,
                        gsub_ref, out_ref, s_ref, mx_ref, acc_ref):
    qi = pl.program_id(1)
    tq = Q_BLOCK
    lo_lanes = _lane((tq, LANES)) < A_DIM

    qa = qa_ref[...]
    pair0, pair1 = qa[:, 0:LANES], qa[:, LANES:2 * LANES]
    zero = jnp.zeros_like(pair0)
    q = jnp.concatenate([jnp.where(lo_lanes, pair0, zero), jnp.where(lo_lanes, pair1, zero),
                         jnp.where(lo_lanes, zero, pair0), jnp.where(lo_lanes, zero, pair1)], axis=0)

    def bias_a(start):
        i0 = start // LANES
        kb = jnp.concatenate([kbias_ref[0, i0], kbias_ref[0, i0 + 1]], axis=1)
        return jnp.concatenate(
            [jnp.broadcast_to(kb[hh:hh + 1, :], (tq, KV_BLOCK)) for hh in range(A_HEADS)], axis=0)

    o = _softmax_pv(q, lambda st: kva_ref[0, pl.ds(st, KV_BLOCK), 0:LANES],
                    lambda st: kva_ref[0, pl.ds(st, KV_BLOCK), LANES:2 * LANES],
                    bias_a, qi, s_ref, mx_ref, acc_ref)
    out_ref[:, 0:LANES] = jnp.where(lo_lanes, o[0:tq], o[2 * tq:3 * tq]).astype(out_ref.dtype)
    out_ref[:, LANES:2 * LANES] = jnp.where(lo_lanes, o[tq:2 * tq], o[3 * tq:4 * tq]).astype(out_ref.dtype)

    q = jnp.concatenate([qb_ref[:, hh * 256:(hh + 1) * 256] for hh in range(B_HEADS)], axis=0)
    o = _softmax_pv(q, lambda st: kb_ref[0, pl.ds(st, KV_BLOCK), :],
                    lambda st: kb_ref[0, pl.ds(st, KV_BLOCK), 0:LANES],
                    None, qi, s_ref, mx_ref, acc_ref)
    for hh in range(B_HEADS):
        out_ref[:, (2 + hh) * LANES:(3 + hh) * LANES] = o[hh * tq:(hh + 1) * tq].astype(out_ref.dtype)

    qc = qc_ref[...]
    parts = []
    for hh in range(C_HEADS):
        blk = qc[:, hh * LANES:(hh + 1) * LANES]
        parts += [jnp.where(lo_lanes, blk, jnp.zeros_like(blk)), jnp.where(lo_lanes, jnp.zeros_like(blk), blk)]
    q = jnp.concatenate(parts, axis=0)
    o = _softmax_pv(q, lambda st: kvc_ref[0, pl.ds(st, KV_BLOCK), 0:LANES],
                    lambda st: kvc_ref[0, pl.ds(st, KV_BLOCK), LANES:2 * LANES],
                    None, qi, s_ref, mx_ref, acc_ref)
    lam, lam_init = _lam(lam_ref)
    for hh in range(C_HEADS):
        d = o[(2 * hh) * tq:(2 * hh + 1) * tq] - lam * o[(2 * hh + 1) * tq:(2 * hh + 2) * tq]
        d = _rms(d, gsub_ref[...]) * (1.0 - lam_init)
        out_ref[:, (6 + hh) * LANES:(7 + hh) * LANES] = d.astype(out_ref.dtype)


def _prompt_attn(qa, qb, qc, kva16, kb16, kvc16, kbias, p, bsz, lp):
    nq = lp // Q_BLOCK
    nkb = nq // 2 + 1
    qspec = lambda w: pl.BlockSpec((Q_BLOCK, w), lambda b, i: (b * nq + i, 0))
    kspec = pl.BlockSpec((1, lp, 256), lambda b, i: (b, 0, 0))
    full = lambda a: pl.BlockSpec(a.shape, lambda b, i: (0,) * a.ndim)
    seq = lambda a: a.reshape(bsz, lp, a.shape[-1])
    return pl.pallas_call(
        _prompt_attn_kernel,
        grid=(bsz, nq),
        in_specs=[qspec(256), qspec(1024), qspec(512), kspec, kspec, kspec,
                  pl.BlockSpec((1,) + kbias.shape[1:], lambda b, i: (b, 0, 0, 0)),
                  full(p['lam']), full(p['g_subln'])],
        out_specs=qspec(MIX_W),
        out_shape=jax.ShapeDtypeStruct((bsz * lp, MIX_W), BF16),
        scratch_shapes=[pltpu.VMEM((nkb, 8 * Q_BLOCK, KV_BLOCK), F32),
                        pltpu.VMEM((8 * Q_BLOCK, KV_BLOCK), F32),
                        pltpu.VMEM((8 * Q_BLOCK, 2 * LANES), F32)],
        compiler_params=pltpu.CompilerParams(
            dimension_semantics=("parallel", "arbitrary"), vmem_limit_bytes=VMEM_LIMIT_BYTES),
        name="prompt_attn",
    )(qa, qb, qc, seq(kva16), seq(kb16), seq(kvc16), kbias, p['lam'], p['g_subln'])


def _decode_kernel(n_chunks, pt_ref, layer_ref, qa_ref, qb_ref, qc_ref, kva_ref, kb_ref, kvc_ref,
                   cn_ref, lam_ref, gsub_ref, ca_hbm, cl_hbm, cb_hbm, cc_hbm, out_ref,
                   buf_a, buf_b, buf_c, buf_l, new_ref, sem):
    b = pl.program_id(0)
    nseq = pl.num_programs(0)
    layer = layer_ref[0]
    ch = PAGES_PER_CHUNK
    tn = qa_ref.shape[1]

    def chunk_copies(seq, chunk, slot):
        cps = []
        for j in range(ch):
            pg = pt_ref[seq, chunk * ch + j]
            lanes = pl.ds(j * PAGE, PAGE)
            cps.append(pltpu.make_async_copy(ca_hbm.at[layer, pg], buf_a.at[slot, :, lanes], sem.at[slot, 0]))
            cps.append(pltpu.make_async_copy(cb_hbm.at[layer, pg],
                                             buf_b.at[slot, 0:B_KV_LORA + B_ROPE, lanes], sem.at[slot, 1]))
            cps.append(pltpu.make_async_copy(cc_hbm.at[layer, pg],
                                             buf_c.at[slot, pl.ds(j * 2 * PAGE, 2 * PAGE), :], sem.at[slot, 2]))
            cps.append(pltpu.make_async_copy(cl_hbm.at[layer, pg],
                                             buf_l.at[slot, pl.ds(j * SUBLANES, SUBLANES), :], sem.at[slot, 3]))
        return cps

    @pl.when(b == 0)
    def _():
        buf_b[...] = jnp.zeros(buf_b.shape, F32)
        new_ref[...] = jnp.zeros(new_ref.shape, F32)
        for cp in chunk_copies(0, n_chunks - 1, 0):
            cp.start()

    lo_lanes = _lane((tn, LANES)) < A_DIM
    qa = qa_ref[0]
    pair0, pair1 = qa[:, 0:LANES], qa[:, LANES:2 * LANES]
    q_a = jnp.concatenate([jnp.where(lo_lanes, pair0, 0.0), jnp.where(lo_lanes, pair1, 0.0),
                           jnp.where(lo_lanes, 0.0, pair0), jnp.where(lo_lanes, 0.0, pair1)],
                          axis=0).astype(BF16)
    qb = qb_ref[0]
    q_b = jnp.concatenate([qb[:, hh * 256:(hh + 1) * 256] for hh in range(B_HEADS)], axis=0).astype(BF16)
    qc = qc_ref[0]
    parts = []
    for hh in range(C_HEADS):
        blk = qc[:, hh * LANES:(hh + 1) * LANES]
        parts += [jnp.where(lo_lanes, blk, 0.0), jnp.where(lo_lanes, 0.0, blk)]
    q_c = jnp.concatenate(parts, axis=0).astype(BF16)

    new_ref[0, 0:tn, :] = kva_ref[0]
    new_ref[1, 0:tn, :] = kb_ref[0]
    new_ref[2, 0:tn, :] = kvc_ref[0]
    new_ref[3, 0:tn, 0:LANES] = cn_ref[0]

    def first_block(q, s, v_nat):
        r = q.shape[0]
        t = lax.broadcasted_iota(jnp.int32, (r, PAGE), 0) % tn
        s = jnp.where(_lane((r, PAGE)) <= t, s, NEG_INF)
        m = jnp.max(s, axis=-1, keepdims=True)
        pr = jnp.exp(s - m).astype(BF16)
        l = jnp.sum(pr.astype(F32), axis=-1, keepdims=True)
        return m, l, _dot(pr, v_nat)

    rows_a = lax.broadcasted_iota(jnp.int32, (A_HEADS * tn, LANES), 0)
    sel_h = jnp.where(_lane((A_HEADS * tn, LANES)) == LOGF_LANE + rows_a // tn, 1.0, 0.0).astype(BF16)
    hi, mid, lo = _split3(new_ref[3, :, 0:LANES])
    cn_t = _dot_nt(sel_h, hi) + _dot_nt(sel_h, mid) + _dot_nt(sel_h, lo)
    kn = new_ref[0, :, 0:LANES].astype(BF16)
    st_a = first_block(q_a, _dot_nt(q_a, kn) - cn_t, new_ref[0, :, LANES:2 * LANES].astype(BF16))
    kn = new_ref[1].astype(BF16)
    st_b = first_block(q_b, _dot_nt(q_b, kn), kn[:, 0:LANES])
    kn = new_ref[2, :, 0:LANES].astype(BF16)
    st_c = first_block(q_c, _dot_nt(q_c, kn), new_ref[2, :, LANES:2 * LANES].astype(BF16))

    def online(state, s, pv_fn):
        m, l, acc = state
        m_new = jnp.maximum(m, jnp.max(s, axis=-1, keepdims=True))
        alpha = jnp.exp(m - m_new)
        pr = jnp.exp(s - m_new).astype(BF16)
        l = alpha * l + jnp.sum(pr.astype(F32), axis=-1, keepdims=True)
        return m_new, l, alpha * acc + pv_fn(pr)

    tail = jnp.zeros((SUBLANES, LANES), F32)
    lane8 = _lane((ch * SUBLANES, LANES))
    for k in range(n_chunks):
        chunk = n_chunks - 1 - k
        slot = k % 2
        if k + 1 < n_chunks:
            for cp in chunk_copies(b, chunk - 1, 1 - slot):
                cp.start()
        else:
            @pl.when(b + 1 < nseq)
            def _():
                for cp in chunk_copies(b + 1, n_chunks - 1, 1 - slot):
                    cp.start()
        for cp in chunk_copies(b, chunk, slot):
            cp.wait()

        x = buf_l[slot]
        y = x
        d = 1
        while d < PAGE:
            y = y + jnp.where(lane8 < PAGE - d, pltpu.roll(y, PAGE - d, 1), 0.0)
            d *= 2
        excl = y - x
        bias_pages = [None] * ch
        for j in reversed(range(ch)):
            sl = slice(j * SUBLANES, (j + 1) * SUBLANES)
            bias_pages[j] = excl[sl] + tail
            tail = tail + jnp.broadcast_to(y[sl][:, 0:1], (SUBLANES, LANES))
        bias = jnp.concatenate(
            [jnp.concatenate([jnp.broadcast_to(bp[hh:hh + 1, :], (tn, PAGE)) for bp in bias_pages], axis=1)
             for hh in range(A_HEADS)], axis=0)

        kt = buf_a[slot, 0:LANES, :].astype(BF16)
        vt = buf_a[slot, LANES:2 * LANES, :].astype(BF16)
        st_a = online(st_a, _dot(q_a, kt) + bias, lambda pr, vt=vt: _dot_nt(pr, vt))
        kt = buf_b[slot].astype(BF16)
        st_b = online(st_b, _dot(q_b, kt), lambda pr, kt=kt: _dot_nt(pr, kt[0:LANES]))
        kc = buf_c[slot, pl.ds(0, ch * PAGE, stride=2), :].astype(BF16)
        vc = buf_c[slot, pl.ds(1, ch * PAGE, stride=2), :].astype(BF16)
        st_c = online(st_c, _dot_nt(q_c, kc), lambda pr, vc=vc: _dot(pr, vc))

    o = st_a[2] / st_a[1]
    out_ref[0, :, 0:LANES] = jnp.where(lo_lanes, o[0:tn], o[2 * tn:3 * tn]).astype(out_ref.dtype)
    out_ref[0, :, LANES:2 * LANES] = jnp.where(lo_lanes, o[tn:2 * tn], o[3 * tn:4 * tn]).astype(out_ref.dtype)
    o = st_b[2] / st_b[1]
    for hh in range(B_HEADS):
        out_ref[0, :, (2 + hh) * LANES:(3 + hh) * LANES] = o[hh * tn:(hh + 1) * tn].astype(out_ref.dtype)
    o = st_c[2] / st_c[1]
    lam, lam_init = _lam(lam_ref)
    for hh in range(C_HEADS):
        d = o[(2 * hh) * tn:(2 * hh + 1) * tn] - lam * o[(2 * hh + 1) * tn:(2 * hh + 2) * tn]
        d = _rms(d, gsub_ref[...]) * (1.0 - lam_init)
        out_ref[0, :, (6 + hh) * LANES:(7 + hh) * LANES] = d.astype(out_ref.dtype)


def _decode_attn(layer, page_table, qa, qb, qc, kva, kb, kvc, cn, p, caches, db, tn):
    ca, cl, cb, cc = caches
    n_pages = page_table.shape[1]
    ch = PAGES_PER_CHUNK
    assert n_pages % (2 * ch) == 0, "page count must split into an even number of chunks"
    n_chunks = n_pages // ch
    cht = ch * PAGE
    seq = lambda a: a.reshape(db, tn, a.shape[-1])
    tile = lambda w: pl.BlockSpec((1, tn, w), lambda b, pt, ly: (b, 0, 0))
    full = lambda a: pl.BlockSpec(a.shape, lambda b, pt, ly: (0,) * a.ndim)
    hbm = pl.BlockSpec(memory_space=pl.ANY)
    grid_spec = pltpu.PrefetchScalarGridSpec(
        num_scalar_prefetch=2,
        grid=(db,),
        in_specs=[tile(256), tile(1024), tile(512), tile(256), tile(256), tile(256), tile(LANES),
                  full(p['lam']), full(p['g_subln']), hbm, hbm, hbm, hbm],
        out_specs=tile(MIX_W),
        scratch_shapes=[pltpu.VMEM((2, 256, cht), F32), pltpu.VMEM((2, 256, cht), F32),
                        pltpu.VMEM((2, 2 * cht, C_V), F32), pltpu.VMEM((2, ch * SUBLANES, LANES), F32),
                        pltpu.VMEM((4, PAGE, 256), F32), pltpu.SemaphoreType.DMA((2, 4))],
    )
    out = pl.pallas_call(
        functools.partial(_decode_kernel, n_chunks),
        grid_spec=grid_spec,
        out_shape=jax.ShapeDtypeStruct((db, tn, MIX_W), F32),
        compiler_params=pltpu.CompilerParams(
            dimension_semantics=("arbitrary",), vmem_limit_bytes=VMEM_LIMIT_BYTES),
        name="decode_attn",
    )(page_table, layer, seq(qa), seq(qb), seq(qc), seq(kva), seq(kb), seq(kvc), seq(cn),
      p['lam'], p['g_subln'], ca, cl, cb, cc)
    return out.reshape(db * tn, MIX_W)


def _out_mlp_kernel(x_ref, mix_ref, wuv_ref, wo_ref, gmix_ref, gpre_ref, wup_ref, wdn_ref, gpost_ref,
                    out_ref):
    mix = mix_ref[...].astype(BF16)
    ob = _dot(mix[:, 256:768], wuv_ref[...]).astype(BF16)
    d = (_dot(mix[:, 0:256], wo_ref[0:256, :]) + _dot(ob, wo_ref[256:512, :])
         + _dot(mix[:, 768:1280], wo_ref[512:1024, :]))
    x1 = x_ref[...] + _rms(d, gmix_ref[...])
    h = _rms(x1, gpre_ref[...]).astype(BF16)
    d_ff = wup_ref.shape[1]
    step = 1024
    m = jnp.zeros(x1.shape, F32)
    for c in range(d_ff // step):
        u = jnp.maximum(_dot(h, wup_ref[:, c * step:(c + 1) * step]), 0.0)
        m = m + _dot((u * u).astype(BF16), wdn_ref[c * step:(c + 1) * step, :])
    out_ref[...] = x1 + _rms(m, gpost_ref[...])


def _out_mlp(x, mix, p, tm, name):
    rows, dm = x.shape
    row = lambda w: pl.BlockSpec((tm, w), lambda i: (i, 0))
    full = lambda a: pl.BlockSpec(a.shape, lambda i: (0,) * a.ndim, pipeline_mode=pl.Buffered(1))
    ws = [p['w_uv'], p['w_out'], p['g_post_mix'], p['g_pre_mlp'], p['w_up'], p['w_down'], p['g_post_mlp']]
    return pl.pallas_call(
        _out_mlp_kernel,
        grid=(rows // tm,),
        in_specs=[row(dm), row(MIX_W)] + [full(w) for w in ws],
        out_specs=row(dm),
        out_shape=jax.ShapeDtypeStruct((rows, dm), F32),
        compiler_params=pltpu.CompilerParams(
            dimension_semantics=("parallel",), vmem_limit_bytes=VMEM_LIMIT_BYTES),
        name=name,
    )(x, mix, *ws)


def _rotary_tables(pos):
    def one(theta, rot_dim, period):
        half = rot_dim // 2
        inv_freq = theta ** (-2.0 * jnp.arange(half, dtype=F32) / rot_dim)
        ang = pos[:, None] * inv_freq[None, :]
        cos, sin = jnp.cos(ang), jnp.sin(ang)
        pad = jnp.zeros((pos.shape[0], period - rot_dim), F32)
        c = jnp.concatenate([cos, cos, pad + 1.0], axis=1)
        s_next = jnp.concatenate([-sin, jnp.zeros_like(sin), pad], axis=1)
        s_prev = jnp.concatenate([jnp.zeros_like(sin), sin, pad], axis=1)
        rep = LANES // period
        return [jnp.tile(t, (1, rep)) for t in (c, s_next, s_prev)]
    return jnp.stack(one(B_THETA, B_ROPE, B_ROPE) + one(C_THETA, C_ROT, C_DIM))


def _prepare_weights(w_in, b_f, g_cq, g_ckv, w_uq, w_uk, w_uv, lam_q1, lam_k1, lam_q2, lam_k2, g_subln,
                     w_out, g_pre_mix, g_post_mix, g_pre_mlp, g_post_mlp, w_up, w_down):
    depth = w_in.shape[0]
    n_in = w_in.shape[2]
    offs = np.cumsum([0, 256, 128, 128, 4, 256, 128, 32, 512, 128, 128])
    za_q, za_k, za_v, za_f, zb_cq, zb_ckv, zb_kr, zc_q, zc_k, zc_v = [
        np.arange(offs[i], offs[i + 1]) for i in range(10)]
    head = lambda hh: za_q[hh * A_DIM:(hh + 1) * A_DIM]
    zero = np.full((LANES - B_ROPE - A_HEADS,), n_in)
    cols = np.concatenate([head(0), head(2), head(1), head(3), za_k, za_v, zb_cq, zb_ckv, zc_q, zc_k,
                           zc_v, zb_kr, za_f, zero])
    assert cols.shape[0] == N_Z
    w_ext = jnp.concatenate([w_in, jnp.zeros(w_in.shape[:2] + (1,), w_in.dtype)], axis=2)
    w_in_p = jnp.take(w_ext, jnp.asarray(cols), axis=2).astype(BF16)

    per = B_NOPE + B_ROPE
    uq_cols = np.concatenate([np.arange(hh * per, hh * per + B_NOPE) for hh in range(B_HEADS)]
                             + [np.arange(hh * per + B_NOPE, (hh + 1) * per) for hh in range(B_HEADS)])
    w_uq_p = jnp.take(w_uq, jnp.asarray(uq_cols), axis=2).astype(BF16)
    w_uk_bd = jnp.zeros((depth, B_HEADS * B_NOPE, B_HEADS * B_KV_LORA), F32)
    w_uv_bd = jnp.zeros((depth, B_HEADS * B_KV_LORA, B_HEADS * B_V), F32)
    for hh in range(B_HEADS):
        w_uk_bd = w_uk_bd.at[:, hh * B_NOPE:(hh + 1) * B_NOPE, hh * B_KV_LORA:(hh + 1) * B_KV_LORA].set(
            jnp.swapaxes(w_uk[:, :, hh, :], 1, 2))
        w_uv_bd = w_uv_bd.at[:, hh * B_KV_LORA:(hh + 1) * B_KV_LORA, hh * B_V:(hh + 1) * B_V].set(
            w_uv[:, :, hh, :])
    place = np.zeros((LANES, B_HEADS * LANES), np.float32)
    for hh in range(B_HEADS):
        for i in range(B_ROPE):
            place[hh * B_ROPE + i, hh * LANES + i] = 1.0
    out_rows = np.concatenate([np.arange(hh * A_DIM, (hh + 1) * A_DIM) for hh in (0, 2, 1, 3)]
                              + [np.arange(A_HEADS * A_DIM, w_out.shape[1])])
    w_out_p = jnp.take(w_out, jnp.asarray(out_rows), axis=1).astype(BF16)

    b_f_slab = jnp.zeros((depth, 1, LANES), F32).at[:, 0, LOGF_LANE:LOGF_LANE + A_HEADS].set(b_f)
    lam_init = jnp.asarray([0.8 - 0.6 * math.exp(-0.3 * l) for l in range(depth)], F32)
    lam = jnp.zeros((depth, SUBLANES, LANES), F32)
    for i, v in enumerate((lam_q1, lam_k1, lam_q2, lam_k2)):
        lam = lam.at[:, i, 0:C_DIM].set(v.astype(F32))
    lam = lam.at[:, 4, :].set(lam_init[:, None])
    row = lambda g: g.astype(F32)[:, None, :]
    stacked = dict(w_in=w_in_p, b_f=b_f_slab, g_cq=row(g_cq), g_ckv=row(g_ckv), w_uq=w_uq_p,
                   w_uk=w_uk_bd.astype(BF16), w_uv=w_uv_bd.astype(BF16), lam=lam, g_subln=row(g_subln),
                   w_out=w_out_p, g_pre_mix=row(g_pre_mix), g_post_mix=row(g_post_mix),
                   g_pre_mlp=row(g_pre_mlp), g_post_mlp=row(g_post_mlp), w_up=w_up.astype(BF16),
                   w_down=w_down.astype(BF16))
    place = jnp.asarray(place, BF16)
    return [dict({k: v[l] for k, v in stacked.items()}, place=place) for l in range(depth)]


def _row_tile(rows, cap, align=16):
    best = None
    for t in range(align, min(rows, cap) + 1, align):
        if rows % t == 0:
            best = t
    assert best is not None, "no aligned row tile"
    return best


def kernel(x_prompt, x_sample, cache_a_kv, cache_a_logf, cache_b_lat, cache_c_kv, page_table, meta_tokens,
           w_in, b_f, g_cq, g_ckv, w_uq, w_uk, w_uv, lam_q1, lam_k1, lam_q2, lam_k2, g_subln, w_out,
           g_pre_mix, g_post_mix, g_pre_mlp, g_post_mlp, w_up, w_down):
    bsz, seq, dm = x_prompt.shape
    db, tn = x_sample.shape[:2]
    depth = w_in.shape[0]
    n_phys = cache_a_kv.shape[1]
    L = N_META + seq
    lp = -(-L // Q_BLOCK) * Q_BLOCK
    assert lp >= KV_BLOCK and tn == SUBLANES
    tp = page_table.shape[1] * PAGE

    layers = _prepare_weights(w_in, b_f, g_cq, g_ckv, w_uq, w_uk, w_uv, lam_q1, lam_k1, lam_q2, lam_k2,
                              g_subln, w_out, g_pre_mix, g_post_mix, g_pre_mlp, g_post_mlp, w_up, w_down)

    meta = jnp.broadcast_to(meta_tokens[None].astype(x_prompt.dtype), (bsz, N_META, dm))
    xp = jnp.concatenate([meta, x_prompt, jnp.zeros((bsz, lp - L, dm), x_prompt.dtype)], axis=1)
    xp = xp.reshape(bsz * lp, dm)
    xs = x_sample.reshape(db * tn, dm)

    tm_in_p = _row_tile(lp, 640)
    tm_s = _row_tile(db * tn, 256, SUBLANES)
    tm_mlp_p = _row_tile(bsz * lp, 256)
    tabs_p = _rotary_tables(jnp.arange(lp, dtype=F32))
    tabs_s = jnp.tile(_rotary_tables(jnp.arange(tn, dtype=F32) + tp), (1, tm_s // tn, 1))
    tri = jnp.asarray(np.kron(np.eye(tm_s // tn), np.tril(np.ones((tn, tn)))), BF16)

    ca = jnp.transpose(cache_a_kv, (0, 1, 3, 4, 5, 2)).reshape(depth, n_phys, 256, PAGE)
    cl = jnp.pad(jnp.transpose(cache_a_logf, (0, 1, 3, 2)), ((0, 0), (0, 0), (0, SUBLANES - A_HEADS), (0, 0)))
    cb = jnp.transpose(cache_b_lat, (0, 1, 3, 2))
    cc = cache_c_kv.reshape(depth, n_phys, 2 * PAGE, C_V)
    caches = (ca, cl, cb, cc)

    outs_p, outs_s = [], []
    for l in range(depth):
        p = layers[l]
        qa, kva, kva16, lf, qb, kb, kb16, qc, kvc, kvc16 = _in_proj(
            xp, tabs_p, p, tm_in_p, lp // tm_in_p, sample=False)
        kbias = _fox_bias(lf, bsz, lp)
        mix = _prompt_attn(qa, qb, qc, kva16, kb16, kvc16, kbias, p, bsz, lp)
        xp = _out_mlp(xp, mix, p, tm_mlp_p, "out_mlp_prompt")
        outs_p.append((kva, lf, kb, kvc))
        qa, kva, lf, qb, kb, qc, kvc, cn = _in_proj(xs, tabs_s, p, tm_s, 1, sample=True, tri=tri)
        mix = _decode_attn(jnp.full((1,), l, jnp.int32), page_table, qa, qb, qc, kva, kb, kvc, cn, p,
                           caches, db, tn)
        xs = _out_mlp(xs, mix, p, tm_s, "out_mlp_sample")
        outs_s.append((kva, lf, kb, kvc))

    def assemble(outs, nb, t_all, t_keep):
        cut = lambda a, w: jnp.stack([o.reshape(nb, t_all, o.shape[-1])[:, :t_keep, :w] for o in a])
        akv = cut([o[0] for o in outs], 256).reshape(depth, nb, t_keep, 2, A_KV_HEADS, A_DIM)
        alf = jnp.stack([o[1].reshape(nb, t_all, LANES)[:, :t_keep, LOGF_LANE:LOGF_LANE + A_HEADS]
                         for o in outs])
        blat = cut([o[2] for o in outs], B_KV_LORA + B_ROPE)
        ckv = cut([o[3] for o in outs], 256).reshape(depth, nb, t_keep, 2, 1, C_V)
        return akv, alf, blat, ckv

    y_prompt = xp.reshape(bsz, lp, dm)[:, N_META:L]
    y_sample = xs.reshape(db, tn, dm)
    return (y_prompt, y_sample) + assemble(outs_p, bsz, lp, L) + assemble(outs_s, db, tn, tn)
```

```python
import functools
import math

import numpy as np
import jax
import jax.numpy as jnp
from jax import lax
from jax.experimental import pallas as pl
from jax.experimental.pallas import tpu as pltpu

F32 = jnp.float32
BF16 = jnp.bfloat16

LANES = 128
SUBLANES = 8
VMEM_LIMIT_BYTES = 56 * 1024 * 1024

N_META = 16
Q_BLOCK = 128
KV_BLOCK = 2 * Q_BLOCK
PAGE = 128
EPS = 1e-6
NEG_INF = -1e30

A_HEADS, A_KV_HEADS, A_DIM = 4, 2, 64
B_HEADS, B_NOPE, B_ROPE, B_V, B_Q_LORA, B_KV_LORA = 4, 64, 32, 64, 256, 128
B_THETA = 10000.0
C_HEADS, C_DIM = 4, 64
C_V = 2 * C_DIM
C_ROT = C_DIM // 4
C_THETA = 500000.0
A_SCALE = A_DIM ** -0.5
B_SCALE = (B_NOPE + B_ROPE) ** -0.5
C_SCALE = C_DIM ** -0.5

LOGF_LANE = 32
N_Z = 14 * LANES
MIX_W = 256 + 512 + 512
PAGES_PER_CHUNK = 32


def _dot(a, b):
    return jnp.dot(a, b, preferred_element_type=F32)


def _dot_nt(a, b):
    return lax.dot_general(a, b, (((1,), (1,)), ((), ())), preferred_element_type=F32)


def _split3(x):
    hi = x.astype(BF16)
    r = x - hi.astype(F32)
    mid = r.astype(BF16)
    lo = (r - mid.astype(F32)).astype(BF16)
    return hi, mid, lo


def _rms(x, g):
    return x * lax.rsqrt(jnp.mean(x * x, axis=-1, keepdims=True) + EPS) * g


def _rot(x, cos, s_next, s_prev, half):
    n = x.shape[-1]
    return x * cos + pltpu.roll(x, n - half, 1) * s_next + pltpu.roll(x, half, 1) * s_prev


def _lane(shape):
    return lax.broadcasted_iota(jnp.int32, shape, len(shape) - 1)


def _lam(lam_ref):
    t1 = jnp.sum(lam_ref[0:1, :] * lam_ref[1:2, :], axis=-1, keepdims=True)
    t2 = jnp.sum(lam_ref[2:3, :] * lam_ref[3:4, :], axis=-1, keepdims=True)
    lam_init = lam_ref[4:5, 0:1]
    return jnp.exp(t1) - jnp.exp(t2) + lam_init, lam_init


def _in_proj_kernel(sample, x_ref, g_ref, w_ref, tab_ref, bf_ref, gcq_ref, wuq_ref, wuk_ref,
                    place_ref, gckv_ref, *refs):
    if sample:
        tri_ref, qa_ref, kva_ref, lf_ref, qb_ref, kb_ref, qc_ref, kvc_ref, cn_ref = refs
    else:
        (qa_ref, kva_ref, kva16_ref, lf_ref, qb_ref, kb_ref, kb16_ref, qc_ref, kvc_ref,
         kvc16_ref) = refs
    h = _rms(x_ref[...], g_ref[...]).astype(BF16)
    z = _dot(h, w_ref[...])

    def blk(i, n=1):
        return z[:, i * LANES:(i + n) * LANES]

    cos_b, sn_b, sp_b = tab_ref[0], tab_ref[1], tab_ref[2]
    cos_c, sn_c, sp_c = tab_ref[3], tab_ref[4], tab_ref[5]

    qa_ref[...] = (blk(0, 2) * A_SCALE).astype(qa_ref.dtype)
    kva = blk(2, 2)
    kva_ref[...] = kva
    last = blk(13)
    v = last + bf_ref[...]
    lf = jnp.minimum(v, 0.0) - jnp.log1p(jnp.exp(-jnp.abs(v)))
    lf_ref[...] = lf

    cq = _rms(blk(4, 2), gcq_ref[...]).astype(BF16)
    qb = _dot(cq, wuq_ref[...])
    q_lat = _dot(qb[:, 0:256].astype(BF16), wuk_ref[...]) * B_SCALE
    q_rope = _rot(qb[:, 256:384], cos_b, sn_b, sp_b, B_ROPE // 2) * B_SCALE
    q_rope = _dot(q_rope.astype(BF16), place_ref[...])
    for hh in range(B_HEADS):
        qb_ref[:, (2 * hh) * LANES:(2 * hh + 1) * LANES] = (
            q_lat[:, hh * LANES:(hh + 1) * LANES].astype(qb_ref.dtype))
        qb_ref[:, (2 * hh + 1) * LANES:(2 * hh + 2) * LANES] = (
            q_rope[:, hh * LANES:(hh + 1) * LANES].astype(qb_ref.dtype))
    ckv = _rms(blk(6), gckv_ref[...])
    k_rope = _rot(last, cos_b, sn_b, sp_b, B_ROPE // 2)
    k_rope = jnp.where(_lane(k_rope.shape) < B_ROPE, k_rope, 0.0)
    kb_ref[:, 0:LANES] = ckv
    kb_ref[:, LANES:2 * LANES] = k_rope

    for j in range(4):
        qc_ref[:, j * LANES:(j + 1) * LANES] = (
            _rot(blk(7 + j), cos_c, sn_c, sp_c, C_ROT // 2) * C_SCALE).astype(qc_ref.dtype)
    kc = _rot(blk(11), cos_c, sn_c, sp_c, C_ROT // 2)
    vc = blk(12)
    kvc_ref[:, 0:LANES] = kc
    kvc_ref[:, LANES:2 * LANES] = vc

    if sample:
        hi, mid, lo = _split3(lf)
        tri = tri_ref[...]
        cn_ref[...] = _dot(tri, hi) + _dot(tri, mid) + _dot(tri, lo)
    else:
        kva16_ref[...] = kva.astype(BF16)
        kb16_ref[:, 0:LANES] = ckv.astype(BF16)
        kb16_ref[:, LANES:2 * LANES] = k_rope.astype(BF16)
        kvc16_ref[:, 0:LANES] = kc.astype(BF16)
        kvc16_ref[:, LANES:2 * LANES] = vc.astype(BF16)


def _in_proj(x, tabs, p, tm, tabs_period_blocks, sample, tri=None):
    rows = x.shape[0]
    grid = (rows // tm,)
    row = lambda w: pl.BlockSpec((tm, w), lambda i: (i, 0))
    full = lambda a: pl.BlockSpec(a.shape, lambda i: (0,) * a.ndim)
    tab_spec = pl.BlockSpec((6, tm, LANES), lambda i: (0, i % tabs_period_blocks, 0))
    ins = [x, p['g_pre_mix'], p['w_in'], tabs, p['b_f'], p['g_cq'], p['w_uq'], p['w_uk'],
           p['place'], p['g_ckv']]
    in_specs = [row(x.shape[1]), full(p['g_pre_mix']), full(p['w_in']), tab_spec, full(p['b_f']),
                full(p['g_cq']), full(p['w_uq']), full(p['w_uk']), full(p['place']),
                full(p['g_ckv'])]
    sds = lambda w, dt: jax.ShapeDtypeStruct((rows, w), dt)
    if sample:
        ins.append(tri)
        in_specs.append(full(tri))
        outs = [(256, F32), (256, F32), (LANES, F32), (1024, F32), (256, F32), (512, F32),
                (256, F32), (LANES, F32)]
    else:
        outs = [(256, BF16), (256, F32), (256, BF16), (LANES, F32), (1024, BF16), (256, F32),
                (256, BF16), (512, BF16), (256, F32), (256, BF16)]
    return pl.pallas_call(
        functools.partial(_in_proj_kernel, sample),
        grid=grid,
        in_specs=in_specs,
        out_specs=[row(w) for w, _ in outs],
        out_shape=[sds(w, dt) for w, dt in outs],
        compiler_params=pltpu.CompilerParams(
            dimension_semantics=("parallel",), vmem_limit_bytes=VMEM_LIMIT_BYTES),
        name="in_proj_sample" if sample else "in_proj_prompt",
    )(*ins)


def _fox_bias_kernel(lf_ref, upper_ref, ones_ref, out_ref):
    nb = out_ref.shape[1]
    rows = lax.broadcasted_iota(jnp.int32, (SUBLANES, LANES), 0)
    sel = (_lane((SUBLANES, LANES)) == rows + LOGF_LANE) & (rows < A_HEADS)
    sel = jnp.where(sel, 1.0, 0.0).astype(BF16)
    hi, mid, lo = _split3(lf_ref[...])
    lft = _dot_nt(sel, hi) + _dot_nt(sel, mid) + _dot_nt(sel, lo)
    x = jnp.concatenate([lft[:, j * LANES:(j + 1) * LANES] for j in range(nb)], axis=0)
    hi, mid, lo = _split3(x)
    up, ones = upper_ref[...], ones_ref[...]
    cum = _dot(hi, up) + _dot(mid, up) + _dot(lo, up)
    tot = _dot(hi, ones) + _dot(mid, ones) + _dot(lo, ones)
    carry = jnp.zeros((SUBLANES, LANES), F32)
    for j in range(nb):
        sl = slice(j * SUBLANES, (j + 1) * SUBLANES)
        out_ref[0, j] = -(cum[sl] + carry)
        carry = carry + tot[sl]


def _fox_bias(lf, bsz, lp):
    nb = lp // LANES
    upper = jnp.asarray(np.triu(np.ones((LANES, LANES), np.float32)), BF16)
    ones = jnp.ones((LANES, LANES), BF16)
    full = lambda a: pl.BlockSpec(a.shape, lambda b: (0,) * a.ndim)
    return pl.pallas_call(
        _fox_bias_kernel,
        grid=(bsz,),
        in_specs=[pl.BlockSpec((lp, LANES), lambda b: (b, 0)), full(upper), full(ones)],
        out_specs=pl.BlockSpec((1, nb, SUBLANES, LANES), lambda b: (b, 0, 0, 0)),
        out_shape=jax.ShapeDtypeStruct((bsz, nb, SUBLANES, LANES), F32),
        compiler_params=pltpu.CompilerParams(dimension_semantics=("parallel",)),
        name="fox_bias",
    )(lf, upper, ones)


def _softmax_pv(mixers, qi, s_ref, mx_ref, acc_ref):
    nfull = qi // 2
    tail_start = pl.multiple_of(jnp.maximum((qi - 1) * Q_BLOCK, 0), Q_BLOCK)
    tail_lo = nfull * KV_BLOCK
    offs = [0]
    for q, _, _, _ in mixers:
        offs.append(offs[-1] + q.shape[0])
    total = offs[-1]
    mx_ref[0:total, :] = jnp.full((total, LANES), NEG_INF, F32)

    def scores(q, k_at, bias_at, start):
        s = _dot_nt(q, k_at(start))
        if bias_at is not None:
            s = s + bias_at(start)
        return s

    def fold(s):
        return jnp.maximum(s[:, 0:LANES], s[:, LANES:2 * LANES])

    def full_block(j, carry):
        start = pl.multiple_of(j * KV_BLOCK, KV_BLOCK)
        for (q, k_at, _, bias_at), off in zip(mixers, offs):
            rows = slice(off, off + q.shape[0])
            s = scores(q, k_at, bias_at, start)
            s_ref[j, rows, :] = s
            mx_ref[rows, :] = jnp.maximum(mx_ref[rows, :], fold(s))
        return carry

    lax.fori_loop(0, nfull, full_block, 0)

    maxima = []
    for (q, k_at, _, bias_at), off in zip(mixers, offs):
        r = q.shape[0]
        rows = slice(off, off + r)
        s = scores(q, k_at, bias_at, tail_start)
        kpos = tail_start + _lane((r, KV_BLOCK))
        qpos = qi * Q_BLOCK + lax.broadcasted_iota(jnp.int32, (r, KV_BLOCK), 0) % Q_BLOCK
        s = jnp.where(kpos >= tail_lo, jnp.where(kpos <= qpos, s, NEG_INF), NEG_INF)
        s_ref[nfull, rows, :] = s
        maxima.append(jnp.max(jnp.maximum(mx_ref[rows, :], fold(s)), axis=-1, keepdims=True))

    acc_ref[0:total, :] = jnp.zeros((total, 2 * LANES), F32)
    ones = jnp.ones((KV_BLOCK, LANES), BF16)

    def pv_block(j, carry):
        start = jnp.where(j < nfull, j * KV_BLOCK, tail_start)
        start = pl.multiple_of(start, Q_BLOCK)
        for (q, _, v_at, _), off, m in zip(mixers, offs, maxima):
            rows = slice(off, off + q.shape[0])
            pr = jnp.exp(s_ref[j, rows, :] - m).astype(BF16)
            vext = jnp.concatenate([v_at(start), ones], axis=1)
            acc_ref[rows, :] += _dot(pr, vext)
        return carry

    lax.fori_loop(0, nfull + 1, pv_block, 0)
    return [acc_ref[off:off + q.shape[0], 0:LANES] / acc_ref[off:off + q.shape[0], LANES:2 * LANES]
            for (q, _, _, _), off in zip(mixers, offs)]


def _prompt_attn_kernel(qa_ref, qb_ref, qc_ref, kva_ref, kb_ref, kvc_ref, kbias_ref, lam_ref,
                        gsub_ref, out_ref, s_ref, mx_ref, acc_ref):
    qi = pl.program_id(1)
    tq = Q_BLOCK
    lo_lanes = _lane((tq, LANES)) < A_DIM

    qa = qa_ref[...]
    pair0, pair1 = qa[:, 0:LANES], qa[:, LANES:2 * LANES]
    zero = jnp.zeros_like(pair0)
    q_a = jnp.concatenate([jnp.where(lo_lanes, pair0, zero), jnp.where(lo_lanes, pair1, zero),
                           jnp.where(lo_lanes, zero, pair0), jnp.where(lo_lanes, zero, pair1)], axis=0)

    def bias_a(start):
        i0 = start // LANES
        kb = jnp.concatenate([kbias_ref[0, i0], kbias_ref[0, i0 + 1]], axis=1)
        return jnp.concatenate(
            [jnp.broadcast_to(kb[hh:hh + 1, :], (tq, KV_BLOCK)) for hh in range(A_HEADS)], axis=0)

    q_b = jnp.concatenate([qb_ref[:, hh * 256:(hh + 1) * 256] for hh in range(B_HEADS)], axis=0)

    qc = qc_ref[...]
    parts = []
    for hh in range(C_HEADS):
        blk = qc[:, hh * LANES:(hh + 1) * LANES]
        parts += [jnp.where(lo_lanes, blk, jnp.zeros_like(blk)), jnp.where(lo_lanes, jnp.zeros_like(blk), blk)]
    q_c = jnp.concatenate(parts, axis=0)

    mixers = [
        (q_a, lambda st: kva_ref[0, pl.ds(st, KV_BLOCK), 0:LANES],
         lambda st: kva_ref[0, pl.ds(st, KV_BLOCK), LANES:2 * LANES], bias_a),
        (q_b, lambda st: kb_ref[0, pl.ds(st, KV_BLOCK), :],
         lambda st: kb_ref[0, pl.ds(st, KV_BLOCK), 0:LANES], None),
        (q_c, lambda st: kvc_ref[0, pl.ds(st, KV_BLOCK), 0:LANES],
         lambda st: kvc_ref[0, pl.ds(st, KV_BLOCK), LANES:2 * LANES], None),
    ]
    o_a, o_b, o_c = _softmax_pv(mixers, qi, s_ref, mx_ref, acc_ref)

    out_ref[:, 0:LANES] = jnp.where(lo_lanes, o_a[0:tq], o_a[2 * tq:3 * tq]).astype(out_ref.dtype)
    out_ref[:, LANES:2 * LANES] = jnp.where(lo_lanes, o_a[tq:2 * tq], o_a[3 * tq:4 * tq]).astype(out_ref.dtype)
    for hh in range(B_HEADS):
        out_ref[:, (2 + hh) * LANES:(3 + hh) * LANES] = o_b[hh * tq:(hh + 1) * tq].astype(out_ref.dtype)
    lam, lam_init = _lam(lam_ref)
    for hh in range(C_HEADS):
        d = o_c[(2 * hh) * tq:(2 * hh + 1) * tq] - lam * o_c[(2 * hh + 1) * tq:(2 * hh + 2) * tq]
        d = _rms(d, gsub_ref[...]) * (1.0 - lam_init)
        out_ref[:, (6 + hh) * LANES:(7 + hh) * LANES] = d.astype(out_ref.dtype)


def _prompt_attn(qa, qb, qc, kva16, kb16, kvc16, kbias, p, bsz, lp):
    nq = lp // Q_BLOCK
    nkb = nq // 2 + 1
    stacked_rows = (A_HEADS + B_HEADS + 2 * C_HEADS) * Q_BLOCK
    qspec = lambda w: pl.BlockSpec((Q_BLOCK, w), lambda b, i: (b * nq + i, 0))
    kspec = pl.BlockSpec((1, lp, 256), lambda b, i: (b, 0, 0))
    full = lambda a: pl.BlockSpec(a.shape, lambda b, i: (0,) * a.ndim)
    seq = lambda a: a.reshape(bsz, lp, a.shape[-1])
    return pl.pallas_call(
        _prompt_attn_kernel,
        grid=(bsz, nq),
        in_specs=[qspec(256), qspec(1024), qspec(512), kspec, kspec, kspec,
                  pl.BlockSpec((1,) + kbias.shape[1:], lambda b, i: (b, 0, 0, 0)),
                  full(p['lam']), full(p['g_subln'])],
        out_specs=qspec(MIX_W),
        out_shape=jax.ShapeDtypeStruct((bsz * lp, MIX_W), BF16),
        scratch_shapes=[pltpu.VMEM((nkb, stacked_rows, KV_BLOCK), F32),
                        pltpu.VMEM((stacked_rows, LANES), F32),
                        pltpu.VMEM((stacked_rows, 2 * LANES), F32)],
        compiler_params=pltpu.CompilerParams(
            dimension_semantics=("parallel", "arbitrary"), vmem_limit_bytes=VMEM_LIMIT_BYTES),
        name="prompt_attn",
    )(qa, qb, qc, seq(kva16), seq(kb16), seq(kvc16), kbias, p['lam'], p['g_subln'])


def _decode_kernel(n_chunks, pt_ref, layer_ref, qa_ref, qb_ref, qc_ref, kva_ref, kb_ref, kvc_ref,
                   cn_ref, lam_ref, gsub_ref, ca_hbm, cl_hbm, cb_hbm, cc_hbm, out_ref,
                   buf_a, buf_b, buf_c, buf_l, new_ref, sem):
    b = pl.program_id(0)
    nseq = pl.num_programs(0)
    layer = layer_ref[0]
    ch = PAGES_PER_CHUNK
    tn = qa_ref.shape[1]

    def chunk_copies(seq, chunk, slot):
        cps = []
        for j in range(ch):
            pg = pt_ref[seq, chunk * ch + j]
            lanes = pl.ds(j * PAGE, PAGE)
            cps.append(pltpu.make_async_copy(ca_hbm.at[layer, pg], buf_a.at[slot, :, lanes], sem.at[slot, 0]))
            cps.append(pltpu.make_async_copy(cb_hbm.at[layer, pg],
                                             buf_b.at[slot, 0:B_KV_LORA + B_ROPE, lanes], sem.at[slot, 1]))
            cps.append(pltpu.make_async_copy(cc_hbm.at[layer, pg],
                                             buf_c.at[slot, pl.ds(j * 2 * PAGE, 2 * PAGE), :], sem.at[slot, 2]))
            cps.append(pltpu.make_async_copy(cl_hbm.at[layer, pg],
                                             buf_l.at[slot, pl.ds(j * SUBLANES, SUBLANES), :], sem.at[slot, 3]))
        return cps

    @pl.when(b == 0)
    def _():
        buf_b[...] = jnp.zeros(buf_b.shape, F32)
        new_ref[...] = jnp.zeros(new_ref.shape, F32)
        for cp in chunk_copies(0, n_chunks - 1, 0):
            cp.start()

    lo_lanes = _lane((tn, LANES)) < A_DIM
    qa = qa_ref[0]
    pair0, pair1 = qa[:, 0:LANES], qa[:, LANES:2 * LANES]
    q_a = jnp.concatenate([jnp.where(lo_lanes, pair0, 0.0), jnp.where(lo_lanes, pair1, 0.0),
                           jnp.where(lo_lanes, 0.0, pair0), jnp.where(lo_lanes, 0.0, pair1)],
                          axis=0).astype(BF16)
    qb = qb_ref[0]
    q_b = jnp.concatenate([qb[:, hh * 256:(hh + 1) * 256] for hh in range(B_HEADS)], axis=0).astype(BF16)
    qc = qc_ref[0]
    parts = []
    for hh in range(C_HEADS):
        blk = qc[:, hh * LANES:(hh + 1) * LANES]
        parts += [jnp.where(lo_lanes, blk, 0.0), jnp.where(lo_lanes, 0.0, blk)]
    q_c = jnp.concatenate(parts, axis=0).astype(BF16)

    new_ref[0, 0:tn, :] = kva_ref[0]
    new_ref[1, 0:tn, :] = kb_ref[0]
    new_ref[2, 0:tn, :] = kvc_ref[0]
    new_ref[3, 0:tn, 0:LANES] = cn_ref[0]

    def first_block(q, s, v_nat):
        r = q.shape[0]
        t = lax.broadcasted_iota(jnp.int32, (r, PAGE), 0) % tn
        s = jnp.where(_lane((r, PAGE)) <= t, s, NEG_INF)
        m = jnp.max(s, axis=-1, keepdims=True)
        pr = jnp.exp(s - m).astype(BF16)
        l = jnp.sum(pr.astype(F32), axis=-1, keepdims=True)
        return m, l, _dot(pr, v_nat)

    rows_a = lax.broadcasted_iota(jnp.int32, (A_HEADS * tn, LANES), 0)
    sel_h = jnp.where(_lane((A_HEADS * tn, LANES)) == LOGF_LANE + rows_a // tn, 1.0, 0.0).astype(BF16)
    hi, mid, lo = _split3(new_ref[3, :, 0:LANES])
    cn_t = _dot_nt(sel_h, hi) + _dot_nt(sel_h, mid) + _dot_nt(sel_h, lo)
    kn = new_ref[0, :, 0:LANES].astype(BF16)
    st_a = first_block(q_a, _dot_nt(q_a, kn) - cn_t, new_ref[0, :, LANES:2 * LANES].astype(BF16))
    kn = new_ref[1].astype(BF16)
    st_b = first_block(q_b, _dot_nt(q_b, kn), kn[:, 0:LANES])
    kn = new_ref[2, :, 0:LANES].astype(BF16)
    st_c = first_block(q_c, _dot_nt(q_c, kn), new_ref[2, :, LANES:2 * LANES].astype(BF16))

    def online(state, s, pv_fn):
        m, l, acc = state
        m_new = jnp.maximum(m, jnp.max(s, axis=-1, keepdims=True))
        alpha = jnp.exp(m - m_new)
        pr = jnp.exp(s - m_new).astype(BF16)
        l = alpha * l + jnp.sum(pr.astype(F32), axis=-1, keepdims=True)
        return m_new, l, alpha * acc + pv_fn(pr)

    tail = jnp.zeros((SUBLANES, LANES), F32)
    lane8 = _lane((ch * SUBLANES, LANES))
    for k in range(n_chunks):
        chunk = n_chunks - 1 - k
        slot = k % 2
        if k + 1 < n_chunks:
            for cp in chunk_copies(b, chunk - 1, 1 - slot):
                cp.start()
        else:
            @pl.when(b + 1 < nseq)
            def _():
                for cp in chunk_copies(b + 1, n_chunks - 1, 1 - slot):
                    cp.start()
        for cp in chunk_copies(b, chunk, slot):
            cp.wait()

        x = buf_l[slot]
        y = x
        d = 1
        while d < PAGE:
            y = y + jnp.where(lane8 < PAGE - d, pltpu.roll(y, PAGE - d, 1), 0.0)
            d *= 2
        excl = y - x
        bias_pages = [None] * ch
        for j in reversed(range(ch)):
            sl = slice(j * SUBLANES, (j + 1) * SUBLANES)
            bias_pages[j] = excl[sl] + tail
            tail = tail + jnp.broadcast_to(y[sl][:, 0:1], (SUBLANES, LANES))
        bias = jnp.concatenate(
            [jnp.concatenate([jnp.broadcast_to(bp[hh:hh + 1, :], (tn, PAGE)) for bp in bias_pages], axis=1)
             for hh in range(A_HEADS)], axis=0)

        kt = buf_a[slot, 0:LANES, :].astype(BF16)
        vt = buf_a[slot, LANES:2 * LANES, :].astype(BF16)
        st_a = online(st_a, _dot(q_a, kt) + bias, lambda pr, vt=vt: _dot_nt(pr, vt))
        kt = buf_b[slot].astype(BF16)
        st_b = online(st_b, _dot(q_b, kt), lambda pr, kt=kt: _dot_nt(pr, kt[0:LANES]))
        kc = buf_c[slot, pl.ds(0, ch * PAGE, stride=2), :].astype(BF16)
        vc = buf_c[slot, pl.ds(1, ch * PAGE, stride=2), :].astype(BF16)
        st_c = online(st_c, _dot_nt(q_c, kc), lambda pr, vc=vc: _dot(pr, vc))

    o = st_a[2] / st_a[1]
    out_ref[0, :, 0:LANES] = jnp.where(lo_lanes, o[0:tn], o[2 * tn:3 * tn]).astype(out_ref.dtype)
    out_ref[0, :, LANES:2 * LANES] = jnp.where(lo_lanes, o[tn:2 * tn], o[3 * tn:4 * tn]).astype(out_ref.dtype)
    o = st_b[2] / st_b[1]
    for hh in range(B_HEADS):
        out_ref[0, :, (2 + hh) * LANES:(3 + hh) * LANES] = o[hh * tn:(hh + 1) * tn].astype(out_ref.dtype)
    o = st_c[2] / st_c[1]
    lam, lam_init = _lam(lam_ref)
    for hh in range(C_HEADS):
        d = o[(2 * hh) * tn:(2 * hh + 1) * tn] - lam * o[(2 * hh + 1) * tn:(2 * hh + 2) * tn]
        d = _rms(d, gsub_ref[...]) * (1.0 - lam_init)
        out_ref[0, :, (6 + hh) * LANES:(7 + hh) * LANES] = d.astype(out_ref.dtype)


def _decode_attn(layer, page_table, qa, qb, qc, kva, kb, kvc, cn, p, caches, db, tn):
    ca, cl, cb, cc = caches
    n_pages = page_table.shape[1]
    ch = PAGES_PER_CHUNK
    assert n_pages % (2 * ch) == 0, "page count must split into an even number of chunks"
    n_chunks = n_pages // ch
    cht = ch * PAGE
    seq = lambda a: a.reshape(db, tn, a.shape[-1])
    tile = lambda w: pl.BlockSpec((1, tn, w), lambda b, pt, ly: (b, 0, 0))
    full = lambda a: pl.BlockSpec(a.shape, lambda b, pt, ly: (0,) * a.ndim)
    hbm = pl.BlockSpec(memory_space=pl.ANY)
    grid_spec = pltpu.PrefetchScalarGridSpec(
        num_scalar_prefetch=2,
        grid=(db,),
        in_specs=[tile(256), tile(1024), tile(512), tile(256), tile(256), tile(256), tile(LANES),
                  full(p['lam']), full(p['g_subln']), hbm, hbm, hbm, hbm],
        out_specs=tile(MIX_W),
        scratch_shapes=[pltpu.VMEM((2, 256, cht), F32), pltpu.VMEM((2, 256, cht), F32),
                        pltpu.VMEM((2, 2 * cht, C_V), F32), pltpu.VMEM((2, ch * SUBLANES, LANES), F32),
                        pltpu.VMEM((4, PAGE, 256), F32), pltpu.SemaphoreType.DMA((2, 4))],
    )
    out = pl.pallas_call(
        functools.partial(_decode_kernel, n_chunks),
        grid_spec=grid_spec,
        out_shape=jax.ShapeDtypeStruct((db, tn, MIX_W), F32),
        compiler_params=pltpu.CompilerParams(
            dimension_semantics=("arbitrary",), vmem_limit_bytes=VMEM_LIMIT_BYTES),
        name="decode_attn",
    )(page_table, layer, seq(qa), seq(qb), seq(qc), seq(kva), seq(kb), seq(kvc), seq(cn),
      p['lam'], p['g_subln'], ca, cl, cb, cc)
    return out.reshape(db * tn, MIX_W)


def _out_mlp_kernel(x_ref, mix_ref, wuv_ref, wo_ref, gmix_ref, gpre_ref, wup_ref, wdn_ref, gpost_ref,
                    out_ref):
    mix = mix_ref[...].astype(BF16)
    ob = _dot(mix[:, 256:768], wuv_ref[...]).astype(BF16)
    d = (_dot(mix[:, 0:256], wo_ref[0:256, :]) + _dot(ob, wo_ref[256:512, :])
         + _dot(mix[:, 768:1280], wo_ref[512:1024, :]))
    x1 = x_ref[...] + _rms(d, gmix_ref[...])
    h = _rms(x1, gpre_ref[...]).astype(BF16)
    d_ff = wup_ref.shape[1]
    step = 1024
    m = jnp.zeros(x1.shape, F32)
    for c in range(d_ff // step):
        u = jnp.maximum(_dot(h, wup_ref[:, c * step:(c + 1) * step]), 0.0)
        m = m + _dot((u * u).astype(BF16), wdn_ref[c * step:(c + 1) * step, :])
    out_ref[...] = x1 + _rms(m, gpost_ref[...])


def _out_mlp(x, mix, p, tm, name):
    rows, dm = x.shape
    row = lambda w: pl.BlockSpec((tm, w), lambda i: (i, 0))
    full = lambda a: pl.BlockSpec(a.shape, lambda i: (0,) * a.ndim, pipeline_mode=pl.Buffered(1))
    ws = [p['w_uv'], p['w_out'], p['g_post_mix'], p['g_pre_mlp'], p['w_up'], p['w_down'], p['g_post_mlp']]
    return pl.pallas_call(
        _out_mlp_kernel,
        grid=(rows // tm,),
        in_specs=[row(dm), row(MIX_W)] + [full(w) for w in ws],
        out_specs=row(dm),
        out_shape=jax.ShapeDtypeStruct((rows, dm), F32),
        compiler_params=pltpu.CompilerParams(
            dimension_semantics=("parallel",), vmem_limit_bytes=VMEM_LIMIT_BYTES),
        name=name,
    )(x, mix, *ws)


def _rotary_tables(pos):
    def one(theta, rot_dim, period):
        half = rot_dim // 2
        inv_freq = theta ** (-2.0 * jnp.arange(half, dtype=F32) / rot_dim)
        ang = pos[:, None] * inv_freq[None, :]
        cos, sin = jnp.cos(ang), jnp.sin(ang)
        pad = jnp.zeros((pos.shape[0], period - rot_dim), F32)
        c = jnp.concatenate([cos, cos, pad + 1.0], axis=1)
        s_next = jnp.concatenate([-sin, jnp.zeros_like(sin), pad], axis=1)
        s_prev = jnp.concatenate([jnp.zeros_like(sin), sin, pad], axis=1)
        rep = LANES // period
        return [jnp.tile(t, (1, rep)) for t in (c, s_next, s_prev)]
    return jnp.stack(one(B_THETA, B_ROPE, B_ROPE) + one(C_THETA, C_ROT, C_DIM))


def _prepare_weights(w_in, b_f, g_cq, g_ckv, w_uq, w_uk, w_uv, lam_q1, lam_k1, lam_q2, lam_k2, g_subln,
                     w_out, g_pre_mix, g_post_mix, g_pre_mlp, g_post_mlp, w_up, w_down):
    depth = w_in.shape[0]
    n_in = w_in.shape[2]
    offs = np.cumsum([0, 256, 128, 128, 4, 256, 128, 32, 512, 128, 128])
    za_q, za_k, za_v, za_f, zb_cq, zb_ckv, zb_kr, zc_q, zc_k, zc_v = [
        np.arange(offs[i], offs[i + 1]) for i in range(10)]
    head = lambda hh: za_q[hh * A_DIM:(hh + 1) * A_DIM]
    zero = np.full((LANES - B_ROPE - A_HEADS,), n_in)
    cols = np.concatenate([head(0), head(2), head(1), head(3), za_k, za_v, zb_cq, zb_ckv, zc_q, zc_k,
                           zc_v, zb_kr, za_f, zero])
    assert cols.shape[0] == N_Z
    w_ext = jnp.concatenate([w_in, jnp.zeros(w_in.shape[:2] + (1,), w_in.dtype)], axis=2)
    w_in_p = jnp.take(w_ext, jnp.asarray(cols), axis=2).astype(BF16)

    per = B_NOPE + B_ROPE
    uq_cols = np.concatenate([np.arange(hh * per, hh * per + B_NOPE) for hh in range(B_HEADS)]
                             + [np.arange(hh * per + B_NOPE, (hh + 1) * per) for hh in range(B_HEADS)])
    w_uq_p = jnp.take(w_uq, jnp.asarray(uq_cols), axis=2).astype(BF16)
    w_uk_bd = jnp.zeros((depth, B_HEADS * B_NOPE, B_HEADS * B_KV_LORA), F32)
    w_uv_bd = jnp.zeros((depth, B_HEADS * B_KV_LORA, B_HEADS * B_V), F32)
    for hh in range(B_HEADS):
        w_uk_bd = w_uk_bd.at[:, hh * B_NOPE:(hh + 1) * B_NOPE, hh * B_KV_LORA:(hh + 1) * B_KV_LORA].set(
            jnp.swapaxes(w_uk[:, :, hh, :], 1, 2))
        w_uv_bd = w_uv_bd.at[:, hh * B_KV_LORA:(hh + 1) * B_KV_LORA, hh * B_V:(hh + 1) * B_V].set(
            w_uv[:, :, hh, :])
    place = np.zeros((LANES, B_HEADS * LANES), np.float32)
    for hh in range(B_HEADS):
        for i in range(B_ROPE):
            place[hh * B_ROPE + i, hh * LANES + i] = 1.0
    out_rows = np.concatenate([np.arange(hh * A_DIM, (hh + 1) * A_DIM) for hh in (0, 2, 1, 3)]
                              + [np.arange(A_HEADS * A_DIM, w_out.shape[1])])
    w_out_p = jnp.take(w_out, jnp.asarray(out_rows), axis=1).astype(BF16)

    b_f_slab = jnp.zeros((depth, 1, LANES), F32).at[:, 0, LOGF_LANE:LOGF_LANE + A_HEADS].set(b_f)
    lam_init = jnp.asarray([0.8 - 0.6 * math.exp(-0.3 * l) for l in range(depth)], F32)
    lam = jnp.zeros((depth, SUBLANES, LANES), F32)
    for i, v in enumerate((lam_q1, lam_k1, lam_q2, lam_k2)):
        lam = lam.at[:, i, 0:C_DIM].set(v.astype(F32))
    lam = lam.at[:, 4, :].set(lam_init[:, None])
    row = lambda g: g.astype(F32)[:, None, :]
    stacked = dict(w_in=w_in_p, b_f=b_f_slab, g_cq=row(g_cq), g_ckv=row(g_ckv), w_uq=w_uq_p,
                   w_uk=w_uk_bd.astype(BF16), w_uv=w_uv_bd.astype(BF16), lam=lam, g_subln=row(g_subln),
                   w_out=w_out_p, g_pre_mix=row(g_pre_mix), g_post_mix=row(g_post_mix),
                   g_pre_mlp=row(g_pre_mlp), g_post_mlp=row(g_post_mlp), w_up=w_up.astype(BF16),
                   w_down=w_down.astype(BF16))
    place = jnp.asarray(place, BF16)
    return [dict({k: v[l] for k, v in stacked.items()}, place=place) for l in range(depth)]


def _row_tile(rows, cap, align=16):
    best = None
    for t in range(align, min(rows, cap) + 1, align):
        if rows % t == 0:
            best = t
    assert best is not None, "no aligned row tile"
    return best


def kernel(x_prompt, x_sample, cache_a_kv, cache_a_logf, cache_b_lat, cache_c_kv, page_table, meta_tokens,
           w_in, b_f, g_cq, g_ckv, w_uq, w_uk, w_uv, lam_q1, lam_k1, lam_q2, lam_k2, g_subln, w_out,
           g_pre_mix, g_post_mix, g_pre_mlp, g_post_mlp, w_up, w_down):
    bsz, seq, dm = x_prompt.shape
    db, tn = x_sample.shape[:2]
    depth = w_in.shape[0]
    n_phys = cache_a_kv.shape[1]
    L = N_META + seq
    lp = -(-L // Q_BLOCK) * Q_BLOCK
    assert lp >= KV_BLOCK and tn == SUBLANES
    tp = page_table.shape[1] * PAGE

    layers = _prepare_weights(w_in, b_f, g_cq, g_ckv, w_uq, w_uk, w_uv, lam_q1, lam_k1, lam_q2, lam_k2,
                              g_subln, w_out, g_pre_mix, g_post_mix, g_pre_mlp, g_post_mlp, w_up, w_down)

    meta = jnp.broadcast_to(meta_tokens[None].astype(x_prompt.dtype), (bsz, N_META, dm))
    xp = jnp.concatenate([meta, x_prompt, jnp.zeros((bsz, lp - L, dm), x_prompt.dtype)], axis=1)
    xp = xp.reshape(bsz * lp, dm)
    xs = x_sample.reshape(db * tn, dm)

    tm_in_p = _row_tile(lp, 640)
    tm_s = _row_tile(db * tn, 256, SUBLANES)
    tm_mlp_p = _row_tile(bsz * lp, 256)
    tabs_p = _rotary_tables(jnp.arange(lp, dtype=F32))
    tabs_s = jnp.tile(_rotary_tables(jnp.arange(tn, dtype=F32) + tp), (1, tm_s // tn, 1))
    tri = jnp.asarray(np.kron(np.eye(tm_s // tn), np.tril(np.ones((tn, tn)))), BF16)

    ca = jnp.transpose(cache_a_kv, (0, 1, 3, 4, 5, 2)).reshape(depth, n_phys, 256, PAGE)
    cl = jnp.pad(jnp.transpose(cache_a_logf, (0, 1, 3, 2)), ((0, 0), (0, 0), (0, SUBLANES - A_HEADS), (0, 0)))
    cb = jnp.transpose(cache_b_lat, (0, 1, 3, 2))
    cc = cache_c_kv.reshape(depth, n_phys, 2 * PAGE, C_V)
    caches = (ca, cl, cb, cc)

    outs_p, outs_s = [], []
    for l in range(depth):
        p = layers[l]
        qa, kva, kva16, lf, qb, kb, kb16, qc, kvc, kvc16 = _in_proj(
            xp, tabs_p, p, tm_in_p, lp // tm_in_p, sample=False)
        kbias = _fox_bias(lf, bsz, lp)
        mix = _prompt_attn(qa, qb, qc, kva16, kb16, kvc16, kbias, p, bsz, lp)
        xp = _out_mlp(xp, mix, p, tm_mlp_p, "out_mlp_prompt")
        outs_p.append((kva, lf, kb, kvc))
        qa, kva, lf, qb, kb, qc, kvc, cn = _in_proj(xs, tabs_s, p, tm_s, 1, sample=True, tri=tri)
        mix = _decode_attn(jnp.full((1,), l, jnp.int32), page_table, qa, qb, qc, kva, kb, kvc, cn, p,
                           caches, db, tn)
        xs = _out_mlp(xs, mix, p, tm_s, "out_mlp_sample")
        outs_s.append((kva, lf, kb, kvc))

    def assemble(outs, nb, t_all, t_keep):
        cut = lambda a, w: jnp.stack([o.reshape(nb, t_all, o.shape[-1])[:, :t_keep, :w] for o in a])
        akv = cut([o[0] for o in outs], 256).reshape(depth, nb, t_keep, 2, A_KV_HEADS, A_DIM)
        alf = jnp.stack([o[1].reshape(nb, t_all, LANES)[:, :t_keep, LOGF_LANE:LOGF_LANE + A_HEADS]
                         for o in outs])
        blat = cut([o[2] for o in outs], B_KV_LORA + B_ROPE)
        ckv = cut([o[3] for o in outs], 256).reshape(depth, nb, t_keep, 2, 1, C_V)
        return akv, alf, blat, ckv

    y_prompt = xp.reshape(bsz, lp, dm)[:, N_META:L]
    y_sample = xs.reshape(db, tn, dm)
    return (y_prompt, y_sample) + assemble(outs_p, bsz, lp, L) + assemble(outs_s, db, tn, tn)
```

```python
import functools
import math

import numpy as np
import jax
import jax.numpy as jnp
from jax import lax
from jax.experimental import pallas as pl
from jax.experimental.pallas import tpu as pltpu

F32 = jnp.float32
BF16 = jnp.bfloat16

LANES = 128
SUBLANES = 8
VMEM_LIMIT_BYTES = 56 * 1024 * 1024

N_META = 16
Q_BLOCK = 128
KV_BLOCK = 2 * Q_BLOCK
PAGE = 128
EPS = 1e-6
NEG_INF = -1e30

A_HEADS, A_KV_HEADS, A_DIM = 4, 2, 64
B_HEADS, B_NOPE, B_ROPE, B_V, B_Q_LORA, B_KV_LORA = 4, 64, 32, 64, 256, 128
B_THETA = 10000.0
C_HEADS, C_DIM = 4, 64
C_V = 2 * C_DIM
C_ROT = C_DIM // 4
C_THETA = 500000.0
LOG2E = math.log2(math.e)
A_SCALE = A_DIM ** -0.5 * LOG2E
B_SCALE = (B_NOPE + B_ROPE) ** -0.5 * LOG2E
C_SCALE = C_DIM ** -0.5 * LOG2E

LOGF_LANE = 32
N_Z = 14 * LANES
MIX_W = 256 + 512 + 512
PAGES_PER_CHUNK = 32


def _dot(a, b):
    return jnp.dot(a, b, preferred_element_type=F32)


def _dot_nt(a, b):
    return lax.dot_general(a, b, (((1,), (1,)), ((), ())), preferred_element_type=F32)


def _split3(x):
    hi = x.astype(BF16)
    r = x - hi.astype(F32)
    mid = r.astype(BF16)
    lo = (r - mid.astype(F32)).astype(BF16)
    return hi, mid, lo


def _rms(x, g):
    return x * lax.rsqrt(jnp.mean(x * x, axis=-1, keepdims=True) + EPS) * g


def _rot(x, cos, s_next, s_prev, half):
    n = x.shape[-1]
    return x * cos + pltpu.roll(x, n - half, 1) * s_next + pltpu.roll(x, half, 1) * s_prev


def _lane(shape):
    return lax.broadcasted_iota(jnp.int32, shape, len(shape) - 1)


def _lam(lam_ref):
    t1 = jnp.sum(lam_ref[0:1, :] * lam_ref[1:2, :], axis=-1, keepdims=True)
    t2 = jnp.sum(lam_ref[2:3, :] * lam_ref[3:4, :], axis=-1, keepdims=True)
    lam_init = lam_ref[4:5, 0:1]
    return jnp.exp(t1) - jnp.exp(t2) + lam_init, lam_init


def _in_proj_kernel(sample, x_ref, g_ref, w_ref, tab_ref, bf_ref, gcq_ref, wuq_ref, wuk_ref,
                    place_ref, gckv_ref, *refs):
    if sample:
        tri_ref, qa_ref, kva_ref, lf_ref, qb_ref, kb_ref, qc_ref, kvc_ref, cn_ref = refs
    else:
        (qa_ref, kva_ref, kva16_ref, lf_ref, qb_ref, kb_ref, kb16_ref, qc_ref, kvc_ref,
         kvc16_ref) = refs
    h = _rms(x_ref[...], g_ref[...]).astype(BF16)
    z = _dot(h, w_ref[...])

    def blk(i, n=1):
        return z[:, i * LANES:(i + n) * LANES]

    cos_b, sn_b, sp_b = tab_ref[0], tab_ref[1], tab_ref[2]
    cos_c, sn_c, sp_c = tab_ref[3], tab_ref[4], tab_ref[5]

    qa_ref[...] = (blk(0, 2) * A_SCALE).astype(qa_ref.dtype)
    kva = blk(2, 2)
    kva_ref[...] = kva
    last = blk(13)
    v = last + bf_ref[...]
    lf = jnp.minimum(v, 0.0) - jnp.log1p(jnp.exp(-jnp.abs(v)))
    lf_ref[...] = lf

    cq = _rms(blk(4, 2), gcq_ref[...]).astype(BF16)
    qb = _dot(cq, wuq_ref[...])
    q_lat = _dot(qb[:, 0:256].astype(BF16), wuk_ref[...]) * B_SCALE
    q_rope = _rot(qb[:, 256:384], cos_b, sn_b, sp_b, B_ROPE // 2) * B_SCALE
    q_rope = _dot(q_rope.astype(BF16), place_ref[...])
    for hh in range(B_HEADS):
        qb_ref[:, (2 * hh) * LANES:(2 * hh + 1) * LANES] = (
            q_lat[:, hh * LANES:(hh + 1) * LANES].astype(qb_ref.dtype))
        qb_ref[:, (2 * hh + 1) * LANES:(2 * hh + 2) * LANES] = (
            q_rope[:, hh * LANES:(hh + 1) * LANES].astype(qb_ref.dtype))
    ckv = _rms(blk(6), gckv_ref[...])
    k_rope = _rot(last, cos_b, sn_b, sp_b, B_ROPE // 2)
    k_rope = jnp.where(_lane(k_rope.shape) < B_ROPE, k_rope, 0.0)
    kb_ref[:, 0:LANES] = ckv
    kb_ref[:, LANES:2 * LANES] = k_rope

    for j in range(4):
        qc_ref[:, j * LANES:(j + 1) * LANES] = (
            _rot(blk(7 + j), cos_c, sn_c, sp_c, C_ROT // 2) * C_SCALE).astype(qc_ref.dtype)
    kc = _rot(blk(11), cos_c, sn_c, sp_c, C_ROT // 2)
    vc = blk(12)
    kvc_ref[:, 0:LANES] = kc
    kvc_ref[:, LANES:2 * LANES] = vc

    if sample:
        hi, mid, lo = _split3(lf)
        tri = tri_ref[...]
        cn_ref[...] = _dot(tri, hi) + _dot(tri, mid) + _dot(tri, lo)
    else:
        kva16_ref[...] = kva.astype(BF16)
        kb16_ref[:, 0:LANES] = ckv.astype(BF16)
        kb16_ref[:, LANES:2 * LANES] = k_rope.astype(BF16)
        kvc16_ref[:, 0:LANES] = kc.astype(BF16)
        kvc16_ref[:, LANES:2 * LANES] = vc.astype(BF16)


def _in_proj(x, tabs, p, tm, tabs_period_blocks, sample, tri=None):
    rows = x.shape[0]
    grid = (rows // tm,)
    row = lambda w: pl.BlockSpec((tm, w), lambda i: (i, 0))
    full = lambda a: pl.BlockSpec(a.shape, lambda i: (0,) * a.ndim)
    tab_spec = pl.BlockSpec((6, tm, LANES), lambda i: (0, i % tabs_period_blocks, 0))
    ins = [x, p['g_pre_mix'], p['w_in'], tabs, p['b_f'], p['g_cq'], p['w_uq'], p['w_uk'],
           p['place'], p['g_ckv']]
    in_specs = [row(x.shape[1]), full(p['g_pre_mix']), full(p['w_in']), tab_spec, full(p['b_f']),
                full(p['g_cq']), full(p['w_uq']), full(p['w_uk']), full(p['place']),
                full(p['g_ckv'])]
    sds = lambda w, dt: jax.ShapeDtypeStruct((rows, w), dt)
    if sample:
        ins.append(tri)
        in_specs.append(full(tri))
        outs = [(256, F32), (256, F32), (LANES, F32), (1024, F32), (256, F32), (512, F32),
                (256, F32), (LANES, F32)]
    else:
        outs = [(256, BF16), (256, F32), (256, BF16), (LANES, F32), (1024, BF16), (256, F32),
                (256, BF16), (512, BF16), (256, F32), (256, BF16)]
    return pl.pallas_call(
        functools.partial(_in_proj_kernel, sample),
        grid=grid,
        in_specs=in_specs,
        out_specs=[row(w) for w, _ in outs],
        out_shape=[sds(w, dt) for w, dt in outs],
        compiler_params=pltpu.CompilerParams(
            dimension_semantics=("parallel",), vmem_limit_bytes=VMEM_LIMIT_BYTES),
        name="in_proj_sample" if sample else "in_proj_prompt",
    )(*ins)


def _fox_bias_kernel(lf_ref, upper_ref, ones_ref, out_ref):
    nb = out_ref.shape[1]
    rows = lax.broadcasted_iota(jnp.int32, (SUBLANES, LANES), 0)
    sel = (_lane((SUBLANES, LANES)) == rows + LOGF_LANE) & (rows < A_HEADS)
    sel = jnp.where(sel, 1.0, 0.0).astype(BF16)
    hi, mid, lo = _split3(lf_ref[...])
    lft = _dot_nt(sel, hi) + _dot_nt(sel, mid) + _dot_nt(sel, lo)
    x = jnp.concatenate([lft[:, j * LANES:(j + 1) * LANES] for j in range(nb)], axis=0)
    hi, mid, lo = _split3(x)
    up, ones = upper_ref[...], ones_ref[...]
    cum = _dot(hi, up) + _dot(mid, up) + _dot(lo, up)
    tot = _dot(hi, ones) + _dot(mid, ones) + _dot(lo, ones)
    carry = jnp.zeros((SUBLANES, LANES), F32)
    for j in range(nb):
        sl = slice(j * SUBLANES, (j + 1) * SUBLANES)
        out_ref[0, j] = -LOG2E * (cum[sl] + carry)
        carry = carry + tot[sl]


def _fox_bias(lf, bsz, lp):
    nb = lp // LANES
    upper = jnp.asarray(np.triu(np.ones((LANES, LANES), np.float32)), BF16)
    ones = jnp.ones((LANES, LANES), BF16)
    full = lambda a: pl.BlockSpec(a.shape, lambda b: (0,) * a.ndim)
    return pl.pallas_call(
        _fox_bias_kernel,
        grid=(bsz,),
        in_specs=[pl.BlockSpec((lp, LANES), lambda b: (b, 0)), full(upper), full(ones)],
        out_specs=pl.BlockSpec((1, nb, SUBLANES, LANES), lambda b: (b, 0, 0, 0)),
        out_shape=jax.ShapeDtypeStruct((bsz, nb, SUBLANES, LANES), F32),
        compiler_params=pltpu.CompilerParams(dimension_semantics=("parallel",)),
        name="fox_bias",
    )(lf, upper, ones)


def _paired_loop(n, body):
    def two(i, carry):
        body(2 * i)
        body(2 * i + 1)
        return carry

    def one(j, carry):
        body(j)
        return carry

    lax.fori_loop(0, n // 2, two, 0)
    lax.fori_loop(2 * (n // 2), n, one, 0)


def _softmax_pv(mixers, qi, s_ref, mx_ref, acc_ref):
    nfull = qi // 2
    tail_start = pl.multiple_of(jnp.maximum((qi - 1) * Q_BLOCK, 0), Q_BLOCK)
    tail_lo = nfull * KV_BLOCK
    offs = [0]
    for q, _, _, _ in mixers:
        offs.append(offs[-1] + q.shape[0])
    total = offs[-1]
    mx_ref[0:total, :] = jnp.full((total, LANES), NEG_INF, F32)

    def scores(q, k_at, bias_at, start):
        s = _dot_nt(q, k_at(start))
        if bias_at is not None:
            s = s + bias_at(start)
        return s

    def fold(s):
        return jnp.maximum(s[:, 0:LANES], s[:, LANES:2 * LANES])

    def full_block(j):
        start = pl.multiple_of(j * KV_BLOCK, KV_BLOCK)
        for (q, k_at, _, bias_at), off in zip(mixers, offs):
            rows = slice(off, off + q.shape[0])
            s = scores(q, k_at, bias_at, start)
            s_ref[j, rows, :] = s
            mx_ref[rows, :] = jnp.maximum(mx_ref[rows, :], fold(s))

    _paired_loop(nfull, full_block)

    maxima = []
    for (q, k_at, _, bias_at), off in zip(mixers, offs):
        r = q.shape[0]
        rows = slice(off, off + r)
        s = scores(q, k_at, bias_at, tail_start)
        kpos = tail_start + _lane((r, KV_BLOCK))
        qpos = qi * Q_BLOCK + lax.broadcasted_iota(jnp.int32, (r, KV_BLOCK), 0) % Q_BLOCK
        s = jnp.where(kpos >= tail_lo, jnp.where(kpos <= qpos, s, NEG_INF), NEG_INF)
        s_ref[nfull, rows, :] = s
        maxima.append(jnp.max(jnp.maximum(mx_ref[rows, :], fold(s)), axis=-1, keepdims=True))

    acc_ref[0:total, :] = jnp.zeros((total, 2 * LANES), F32)
    ones = jnp.ones((KV_BLOCK, LANES), BF16)

    def pv_block(j):
        start = jnp.where(j < nfull, j * KV_BLOCK, tail_start)
        start = pl.multiple_of(start, Q_BLOCK)
        for (q, _, v_at, _), off, m in zip(mixers, offs, maxima):
            rows = slice(off, off + q.shape[0])
            pr = jnp.exp2(s_ref[j, rows, :] - m).astype(BF16)
            vext = jnp.concatenate([v_at(start), ones], axis=1)
            acc_ref[rows, :] += _dot(pr, vext)

    _paired_loop(nfull + 1, pv_block)
    return [acc_ref[off:off + q.shape[0], 0:LANES] / acc_ref[off:off + q.shape[0], LANES:2 * LANES]
            for (q, _, _, _), off in zip(mixers, offs)]


def _prompt_attn_kernel(qa_ref, qb_ref, qc_ref, kva_ref, kb_ref, kvc_ref, kbias_ref, lam_ref,
                        gsub_ref, out_ref, s_ref, mx_ref, acc_ref):
    qi = pl.program_id(1)
    tq = Q_BLOCK
    lo_lanes = _lane((tq, LANES)) < A_DIM

    qa = qa_ref[...]
    pair0, pair1 = qa[:, 0:LANES], qa[:, LANES:2 * LANES]
    zero = jnp.zeros_like(pair0)
    q_a = jnp.concatenate([jnp.where(lo_lanes, pair0, zero), jnp.where(lo_lanes, pair1, zero),
                           jnp.where(lo_lanes, zero, pair0), jnp.where(lo_lanes, zero, pair1)], axis=0)

    def bias_a(start):
        i0 = start // LANES
        kb = jnp.concatenate([kbias_ref[0, i0], kbias_ref[0, i0 + 1]], axis=1)
        return jnp.concatenate(
            [jnp.broadcast_to(kb[hh:hh + 1, :], (tq, KV_BLOCK)) for hh in range(A_HEADS)], axis=0)

    q_b = jnp.concatenate([qb_ref[:, hh * 256:(hh + 1) * 256] for hh in range(B_HEADS)], axis=0)

    qc = qc_ref[...]
    parts = []
    for hh in range(C_HEADS):
        blk = qc[:, hh * LANES:(hh + 1) * LANES]
        parts += [jnp.where(lo_lanes, blk, jnp.zeros_like(blk)), jnp.where(lo_lanes, jnp.zeros_like(blk), blk)]
    q_c = jnp.concatenate(parts, axis=0)

    mixers = [
        (q_a, lambda st: kva_ref[0, pl.ds(st, KV_BLOCK), 0:LANES],
         lambda st: kva_ref[0, pl.ds(st, KV_BLOCK), LANES:2 * LANES], bias_a),
        (q_b, lambda st: kb_ref[0, pl.ds(st, KV_BLOCK), :],
         lambda st: kb_ref[0, pl.ds(st, KV_BLOCK), 0:LANES], None),
        (q_c, lambda st: kvc_ref[0, pl.ds(st, KV_BLOCK), 0:LANES],
         lambda st: kvc_ref[0, pl.ds(st, KV_BLOCK), LANES:2 * LANES], None),
    ]
    o_a, o_b, o_c = _softmax_pv(mixers, qi, s_ref, mx_ref, acc_ref)

    out_ref[:, 0:LANES] = jnp.where(lo_lanes, o_a[0:tq], o_a[2 * tq:3 * tq]).astype(out_ref.dtype)
    out_ref[:, LANES:2 * LANES] = jnp.where(lo_lanes, o_a[tq:2 * tq], o_a[3 * tq:4 * tq]).astype(out_ref.dtype)
    for hh in range(B_HEADS):
        out_ref[:, (2 + hh) * LANES:(3 + hh) * LANES] = o_b[hh * tq:(hh + 1) * tq].astype(out_ref.dtype)
    lam, lam_init = _lam(lam_ref)
    for hh in range(C_HEADS):
        d = o_c[(2 * hh) * tq:(2 * hh + 1) * tq] - lam * o_c[(2 * hh + 1) * tq:(2 * hh + 2) * tq]
        d = _rms(d, gsub_ref[...]) * (1.0 - lam_init)
        out_ref[:, (6 + hh) * LANES:(7 + hh) * LANES] = d.astype(out_ref.dtype)


def _prompt_attn(qa, qb, qc, kva16, kb16, kvc16, kbias, p, bsz, lp):
    nq = lp // Q_BLOCK
    nkb = nq // 2 + 1
    stacked_rows = (A_HEADS + B_HEADS + 2 * C_HEADS) * Q_BLOCK
    qspec = lambda w: pl.BlockSpec((Q_BLOCK, w), lambda b, i: (b * nq + i, 0))
    kspec = pl.BlockSpec((1, lp, 256), lambda b, i: (b, 0, 0))
    full = lambda a: pl.BlockSpec(a.shape, lambda b, i: (0,) * a.ndim)
    seq = lambda a: a.reshape(bsz, lp, a.shape[-1])
    return pl.pallas_call(
        _prompt_attn_kernel,
        grid=(bsz, nq),
        in_specs=[qspec(256), qspec(1024), qspec(512), kspec, kspec, kspec,
                  pl.BlockSpec((1,) + kbias.shape[1:], lambda b, i: (b, 0, 0, 0)),
                  full(p['lam']), full(p['g_subln'])],
        out_specs=qspec(MIX_W),
        out_shape=jax.ShapeDtypeStruct((bsz * lp, MIX_W), BF16),
        scratch_shapes=[pltpu.VMEM((nkb, stacked_rows, KV_BLOCK), F32),
                        pltpu.VMEM((stacked_rows, LANES), F32),
                        pltpu.VMEM((stacked_rows, 2 * LANES), F32)],
        compiler_params=pltpu.CompilerParams(
            dimension_semantics=("parallel", "arbitrary"), vmem_limit_bytes=VMEM_LIMIT_BYTES),
        name="prompt_attn",
    )(qa, qb, qc, seq(kva16), seq(kb16), seq(kvc16), kbias, p['lam'], p['g_subln'])


def _decode_kernel(n_chunks, pt_ref, layer_ref, qa_ref, qb_ref, qc_ref, kva_ref, kb_ref, kvc_ref,
                   cn_ref, lam_ref, gsub_ref, ca_hbm, cl_hbm, cb_hbm, cc_hbm, out_ref,
                   buf_a, buf_b, buf_c, buf_l, new_ref, sem):
    b = pl.program_id(0)
    nseq = pl.num_programs(0)
    layer = layer_ref[0]
    ch = PAGES_PER_CHUNK
    tn = qa_ref.shape[1]

    def chunk_copies(seq, chunk, slot):
        cps = []
        for j in range(ch):
            pg = pt_ref[seq, chunk * ch + j]
            lanes = pl.ds(j * PAGE, PAGE)
            cps.append(pltpu.make_async_copy(ca_hbm.at[layer, pg], buf_a.at[slot, :, lanes], sem.at[slot, 0]))
            cps.append(pltpu.make_async_copy(cb_hbm.at[layer, pg],
                                             buf_b.at[slot, 0:B_KV_LORA + B_ROPE, lanes], sem.at[slot, 1]))
            cps.append(pltpu.make_async_copy(cc_hbm.at[layer, pg],
                                             buf_c.at[slot, pl.ds(j * 2 * PAGE, 2 * PAGE), :], sem.at[slot, 2]))
            cps.append(pltpu.make_async_copy(cl_hbm.at[layer, pg],
                                             buf_l.at[slot, pl.ds(j * SUBLANES, SUBLANES), :], sem.at[slot, 3]))
        return cps

    @pl.when(b == 0)
    def _():
        buf_b[...] = jnp.zeros(buf_b.shape, F32)
        new_ref[...] = jnp.zeros(new_ref.shape, F32)
        for cp in chunk_copies(0, n_chunks - 1, 0):
            cp.start()

    lo_lanes = _lane((tn, LANES)) < A_DIM
    qa = qa_ref[0]
    pair0, pair1 = qa[:, 0:LANES], qa[:, LANES:2 * LANES]
    q_a = jnp.concatenate([jnp.where(lo_lanes, pair0, 0.0), jnp.where(lo_lanes, pair1, 0.0),
                           jnp.where(lo_lanes, 0.0, pair0), jnp.where(lo_lanes, 0.0, pair1)],
                          axis=0).astype(BF16)
    qb = qb_ref[0]
    q_b = jnp.concatenate([qb[:, hh * 256:(hh + 1) * 256] for hh in range(B_HEADS)], axis=0).astype(BF16)
    qc = qc_ref[0]
    parts = []
    for hh in range(C_HEADS):
        blk = qc[:, hh * LANES:(hh + 1) * LANES]
        parts += [jnp.where(lo_lanes, blk, 0.0), jnp.where(lo_lanes, 0.0, blk)]
    q_c = jnp.concatenate(parts, axis=0).astype(BF16)

    new_ref[0, 0:tn, :] = kva_ref[0]
    new_ref[1, 0:tn, :] = kb_ref[0]
    new_ref[2, 0:tn, :] = kvc_ref[0]
    new_ref[3, 0:tn, 0:LANES] = cn_ref[0]

    def first_block(q, s, v_nat):
        r = q.shape[0]
        t = lax.broadcasted_iota(jnp.int32, (r, PAGE), 0) % tn
        s = jnp.where(_lane((r, PAGE)) <= t, s, NEG_INF)
        m = jnp.max(s, axis=-1, keepdims=True)
        pr = jnp.exp2(s - m).astype(BF16)
        l = jnp.sum(pr.astype(F32), axis=-1, keepdims=True)
        return m, l, _dot(pr, v_nat)

    rows_a = lax.broadcasted_iota(jnp.int32, (A_HEADS * tn, LANES), 0)
    sel_h = jnp.where(_lane((A_HEADS * tn, LANES)) == LOGF_LANE + rows_a // tn, 1.0, 0.0).astype(BF16)
    hi, mid, lo = _split3(new_ref[3, :, 0:LANES])
    cn_t = _dot_nt(sel_h, hi) + _dot_nt(sel_h, mid) + _dot_nt(sel_h, lo)
    kn = new_ref[0, :, 0:LANES].astype(BF16)
    st_a = first_block(q_a, _dot_nt(q_a, kn) - LOG2E * cn_t, new_ref[0, :, LANES:2 * LANES].astype(BF16))
    kn = new_ref[1].astype(BF16)
    st_b = first_block(q_b, _dot_nt(q_b, kn), kn[:, 0:LANES])
    kn = new_ref[2, :, 0:LANES].astype(BF16)
    st_c = first_block(q_c, _dot_nt(q_c, kn), new_ref[2, :, LANES:2 * LANES].astype(BF16))

    def online(state, s, pv_fn):
        m, l, acc = state
        m_new = jnp.maximum(m, jnp.max(s, axis=-1, keepdims=True))
        alpha = jnp.exp2(m - m_new)
        pr = jnp.exp2(s - m_new).astype(BF16)
        l = alpha * l + jnp.sum(pr.astype(F32), axis=-1, keepdims=True)
        return m_new, l, alpha * acc + pv_fn(pr)

    tail = jnp.zeros((SUBLANES, LANES), F32)
    lane8 = _lane((ch * SUBLANES, LANES))
    for k in range(n_chunks):
        chunk = n_chunks - 1 - k
        slot = k % 2
        if k + 1 < n_chunks:
            for cp in chunk_copies(b, chunk - 1, 1 - slot):
                cp.start()
        else:
            @pl.when(b + 1 < nseq)
            def _():
                for cp in chunk_copies(b + 1, n_chunks - 1, 1 - slot):
                    cp.start()
        for cp in chunk_copies(b, chunk, slot):
            cp.wait()

        x = buf_l[slot]
        y = x
        d = 1
        while d < PAGE:
            y = y + jnp.where(lane8 < PAGE - d, pltpu.roll(y, PAGE - d, 1), 0.0)
            d *= 2
        excl = y - x
        bias_pages = [None] * ch
        for j in reversed(range(ch)):
            sl = slice(j * SUBLANES, (j + 1) * SUBLANES)
            bias_pages[j] = LOG2E * (excl[sl] + tail)
            tail = tail + jnp.broadcast_to(y[sl][:, 0:1], (SUBLANES, LANES))
        bias = jnp.concatenate(
            [jnp.concatenate([jnp.broadcast_to(bp[hh:hh + 1, :], (tn, PAGE)) for bp in bias_pages], axis=1)
             for hh in range(A_HEADS)], axis=0)

        kt = buf_a[slot, 0:LANES, :].astype(BF16)
        vt = buf_a[slot, LANES:2 * LANES, :].astype(BF16)
        st_a = online(st_a, _dot(q_a, kt) + bias, lambda pr, vt=vt: _dot_nt(pr, vt))
        kt = buf_b[slot].astype(BF16)
        st_b = online(st_b, _dot(q_b, kt), lambda pr, kt=kt: _dot_nt(pr, kt[0:LANES]))
        kc = buf_c[slot, pl.ds(0, ch * PAGE, stride=2), :].astype(BF16)
        vc = buf_c[slot, pl.ds(1, ch * PAGE, stride=2), :].astype(BF16)
        st_c = online(st_c, _dot_nt(q_c, kc), lambda pr, vc=vc: _dot(pr, vc))

    o = st_a[2] / st_a[1]
    out_ref[0, :, 0:LANES] = jnp.where(lo_lanes, o[0:tn], o[2 * tn:3 * tn]).astype(out_ref.dtype)
    out_ref[0, :, LANES:2 * LANES] = jnp.where(lo_lanes, o[tn:2 * tn], o[3 * tn:4 * tn]).astype(out_ref.dtype)
    o = st_b[2] / st_b[1]
    for hh in range(B_HEADS):
        out_ref[0, :, (2 + hh) * LANES:(3 + hh) * LANES] = o[hh * tn:(hh + 1) * tn].astype(out_ref.dtype)
    o = st_c[2] / st_c[1]
    lam, lam_init = _lam(lam_ref)
    for hh in range(C_HEADS):
        d = o[(2 * hh) * tn:(2 * hh + 1) * tn] - lam * o[(2 * hh + 1) * tn:(2 * hh + 2) * tn]
        d = _rms(d, gsub_ref[...]) * (1.0 - lam_init)
        out_ref[0, :, (6 + hh) * LANES:(7 + hh) * LANES] = d.astype(out_ref.dtype)


def _decode_attn(layer, page_table, qa, qb, qc, kva, kb, kvc, cn, p, caches, db, tn):
    ca, cl, cb, cc = caches
    n_pages = page_table.shape[1]
    ch = PAGES_PER_CHUNK
    assert n_pages % (2 * ch) == 0, "page count must split into an even number of chunks"
    n_chunks = n_pages // ch
    cht = ch * PAGE
    seq = lambda a: a.reshape(db, tn, a.shape[-1])
    tile = lambda w: pl.BlockSpec((1, tn, w), lambda b, pt, ly: (b, 0, 0))
    full = lambda a: pl.BlockSpec(a.shape, lambda b, pt, ly: (0,) * a.ndim)
    hbm = pl.BlockSpec(memory_space=pl.ANY)
    grid_spec = pltpu.PrefetchScalarGridSpec(
        num_scalar_prefetch=2,
        grid=(db,),
        in_specs=[tile(256), tile(1024), tile(512), tile(256), tile(256), tile(256), tile(LANES),
                  full(p['lam']), full(p['g_subln']), hbm, hbm, hbm, hbm],
        out_specs=tile(MIX_W),
        scratch_shapes=[pltpu.VMEM((2, 256, cht), F32), pltpu.VMEM((2, 256, cht), F32),
                        pltpu.VMEM((2, 2 * cht, C_V), F32), pltpu.VMEM((2, ch * SUBLANES, LANES), F32),
                        pltpu.VMEM((4, PAGE, 256), F32), pltpu.SemaphoreType.DMA((2, 4))],
    )
    out = pl.pallas_call(
        functools.partial(_decode_kernel, n_chunks),
        grid_spec=grid_spec,
        out_shape=jax.ShapeDtypeStruct((db, tn, MIX_W), F32),
        compiler_params=pltpu.CompilerParams(
            dimension_semantics=("arbitrary",), vmem_limit_bytes=VMEM_LIMIT_BYTES),
        name="decode_attn",
    )(page_table, layer, seq(qa), seq(qb), seq(qc), seq(kva), seq(kb), seq(kvc), seq(cn),
      p['lam'], p['g_subln'], ca, cl, cb, cc)
    return out.reshape(db * tn, MIX_W)


def _out_mlp_kernel(x_ref, mix_ref, wuv_ref, wo_ref, gmix_ref, gpre_ref, wup_ref, wdn_ref, gpost_ref,
                    out_ref):
    mix = mix_ref[...].astype(BF16)
    ob = _dot(mix[:, 256:768], wuv_ref[...]).astype(BF16)
    d = (_dot(mix[:, 0:256], wo_ref[0:256, :]) + _dot(ob, wo_ref[256:512, :])
         + _dot(mix[:, 768:1280], wo_ref[512:1024, :]))
    x1 = x_ref[...] + _rms(d, gmix_ref[...])
    h = _rms(x1, gpre_ref[...]).astype(BF16)
    d_ff = wup_ref.shape[1]
    step = 1024
    m = jnp.zeros(x1.shape, F32)
    for c in range(d_ff // step):
        u = jnp.maximum(_dot(h, wup_ref[:, c * step:(c + 1) * step]), 0.0)
        m = m + _dot((u * u).astype(BF16), wdn_ref[c * step:(c + 1) * step, :])
    out_ref[...] = x1 + _rms(m, gpost_ref[...])


def _out_mlp(x, mix, p, tm, name):
    rows, dm = x.shape
    row = lambda w: pl.BlockSpec((tm, w), lambda i: (i, 0))
    full = lambda a: pl.BlockSpec(a.shape, lambda i: (0,) * a.ndim, pipeline_mode=pl.Buffered(1))
    ws = [p['w_uv'], p['w_out'], p['g_post_mix'], p['g_pre_mlp'], p['w_up'], p['w_down'], p['g_post_mlp']]
    return pl.pallas_call(
        _out_mlp_kernel,
        grid=(rows // tm,),
        in_specs=[row(dm), row(MIX_W)] + [full(w) for w in ws],
        out_specs=row(dm),
        out_shape=jax.ShapeDtypeStruct((rows, dm), F32),
        compiler_params=pltpu.CompilerParams(
            dimension_semantics=("parallel",), vmem_limit_bytes=VMEM_LIMIT_BYTES),
        name=name,
    )(x, mix, *ws)


def _rotary_tables(pos):
    def one(theta, rot_dim, period):
        half = rot_dim // 2
        inv_freq = theta ** (-2.0 * jnp.arange(half, dtype=F32) / rot_dim)
        ang = pos[:, None] * inv_freq[None, :]
        cos, sin = jnp.cos(ang), jnp.sin(ang)
        pad = jnp.zeros((pos.shape[0], period - rot_dim), F32)
        c = jnp.concatenate([cos, cos, pad + 1.0], axis=1)
        s_next = jnp.concatenate([-sin, jnp.zeros_like(sin), pad], axis=1)
        s_prev = jnp.concatenate([jnp.zeros_like(sin), sin, pad], axis=1)
        rep = LANES // period
        return [jnp.tile(t, (1, rep)) for t in (c, s_next, s_prev)]
    return jnp.stack(one(B_THETA, B_ROPE, B_ROPE) + one(C_THETA, C_ROT, C_DIM))


def _prepare_weights(w_in, b_f, g_cq, g_ckv, w_uq, w_uk, w_uv, lam_q1, lam_k1, lam_q2, lam_k2, g_subln,
                     w_out, g_pre_mix, g_post_mix, g_pre_mlp, g_post_mlp, w_up, w_down):
    depth = w_in.shape[0]
    n_in = w_in.shape[2]
    offs = np.cumsum([0, 256, 128, 128, 4, 256, 128, 32, 512, 128, 128])
    za_q, za_k, za_v, za_f, zb_cq, zb_ckv, zb_kr, zc_q, zc_k, zc_v = [
        np.arange(offs[i], offs[i + 1]) for i in range(10)]
    head = lambda hh: za_q[hh * A_DIM:(hh + 1) * A_DIM]
    zero = np.full((LANES - B_ROPE - A_HEADS,), n_in)
    cols = np.concatenate([head(0), head(2), head(1), head(3), za_k, za_v, zb_cq, zb_ckv, zc_q, zc_k,
                           zc_v, zb_kr, za_f, zero])
    assert cols.shape[0] == N_Z
    w_ext = jnp.concatenate([w_in, jnp.zeros(w_in.shape[:2] + (1,), w_in.dtype)], axis=2)
    w_in_p = jnp.take(w_ext, jnp.asarray(cols), axis=2).astype(BF16)

    per = B_NOPE + B_ROPE
    uq_cols = np.concatenate([np.arange(hh * per, hh * per + B_NOPE) for hh in range(B_HEADS)]
                             + [np.arange(hh * per + B_NOPE, (hh + 1) * per) for hh in range(B_HEADS)])
    w_uq_p = jnp.take(w_uq, jnp.asarray(uq_cols), axis=2).astype(BF16)
    w_uk_bd = jnp.zeros((depth, B_HEADS * B_NOPE, B_HEADS * B_KV_LORA), F32)
    w_uv_bd = jnp.zeros((depth, B_HEADS * B_KV_LORA, B_HEADS * B_V), F32)
    for hh in range(B_HEADS):
        w_uk_bd = w_uk_bd.at[:, hh * B_NOPE:(hh + 1) * B_NOPE, hh * B_KV_LORA:(hh + 1) * B_KV_LORA].set(
            jnp.swapaxes(w_uk[:, :, hh, :], 1, 2))
        w_uv_bd = w_uv_bd.at[:, hh * B_KV_LORA:(hh + 1) * B_KV_LORA, hh * B_V:(hh + 1) * B_V].set(
            w_uv[:, :, hh, :])
    place = np.zeros((LANES, B_HEADS * LANES), np.float32)
    for hh in range(B_HEADS):
        for i in range(B_ROPE):
            place[hh * B_ROPE + i, hh * LANES + i] = 1.0
    out_rows = np.concatenate([np.arange(hh * A_DIM, (hh + 1) * A_DIM) for hh in (0, 2, 1, 3)]
                              + [np.arange(A_HEADS * A_DIM, w_out.shape[1])])
    w_out_p = jnp.take(w_out, jnp.asarray(out_rows), axis=1).astype(BF16)

    b_f_slab = jnp.zeros((depth, 1, LANES), F32).at[:, 0, LOGF_LANE:LOGF_LANE + A_HEADS].set(b_f)
    lam_init = jnp.asarray([0.8 - 0.6 * math.exp(-0.3 * l) for l in range(depth)], F32)
    lam = jnp.zeros((depth, SUBLANES, LANES), F32)
    for i, v in enumerate((lam_q1, lam_k1, lam_q2, lam_k2)):
        lam = lam.at[:, i, 0:C_DIM].set(v.astype(F32))
    lam = lam.at[:, 4, :].set(lam_init[:, None])
    row = lambda g: g.astype(F32)[:, None, :]
    stacked = dict(w_in=w_in_p, b_f=b_f_slab, g_cq=row(g_cq), g_ckv=row(g_ckv), w_uq=w_uq_p,
                   w_uk=w_uk_bd.astype(BF16), w_uv=w_uv_bd.astype(BF16), lam=lam, g_subln=row(g_subln),
                   w_out=w_out_p, g_pre_mix=row(g_pre_mix), g_post_mix=row(g_post_mix),
                   g_pre_mlp=row(g_pre_mlp), g_post_mlp=row(g_post_mlp), w_up=w_up.astype(BF16),
                   w_down=w_down.astype(BF16))
    place = jnp.asarray(place, BF16)
    return [dict({k: v[l] for k, v in stacked.items()}, place=place) for l in range(depth)]


def _row_tile(rows, cap, align=16):
    best = None
    for t in range(align, min(rows, cap) + 1, align):
        if rows % t == 0:
            best = t
    assert best is not None, "no aligned row tile"
    return best


def kernel(x_prompt, x_sample, cache_a_kv, cache_a_logf, cache_b_lat, cache_c_kv, page_table, meta_tokens,
           w_in, b_f, g_cq, g_ckv, w_uq, w_uk, w_uv, lam_q1, lam_k1, lam_q2, lam_k2, g_subln, w_out,
           g_pre_mix, g_post_mix, g_pre_mlp, g_post_mlp, w_up, w_down):
    bsz, seq, dm = x_prompt.shape
    db, tn = x_sample.shape[:2]
    depth = w_in.shape[0]
    n_phys = cache_a_kv.shape[1]
    L = N_META + seq
    lp = -(-L // Q_BLOCK) * Q_BLOCK
    assert lp >= KV_BLOCK and tn == SUBLANES
    tp = page_table.shape[1] * PAGE

    layers = _prepare_weights(w_in, b_f, g_cq, g_ckv, w_uq, w_uk, w_uv, lam_q1, lam_k1, lam_q2, lam_k2,
                              g_subln, w_out, g_pre_mix, g_post_mix, g_pre_mlp, g_post_mlp, w_up, w_down)

    meta = jnp.broadcast_to(meta_tokens[None].astype(x_prompt.dtype), (bsz, N_META, dm))
    xp = jnp.concatenate([meta, x_prompt, jnp.zeros((bsz, lp - L, dm), x_prompt.dtype)], axis=1)
    xp = xp.reshape(bsz * lp, dm)
    xs = x_sample.reshape(db * tn, dm)

    tm_in_p = _row_tile(lp, 640)
    tm_s = _row_tile(db * tn, 256, SUBLANES)
    tm_mlp_p = _row_tile(bsz * lp, 512)
    tabs_p = _rotary_tables(jnp.arange(lp, dtype=F32))
    tabs_s = jnp.tile(_rotary_tables(jnp.arange(tn, dtype=F32) + tp), (1, tm_s // tn, 1))
    tri = jnp.asarray(np.kron(np.eye(tm_s // tn), np.tril(np.ones((tn, tn)))), BF16)

    ca = jnp.transpose(cache_a_kv, (0, 1, 3, 4, 5, 2)).reshape(depth, n_phys, 256, PAGE)
    cl = jnp.pad(jnp.transpose(cache_a_logf, (0, 1, 3, 2)), ((0, 0), (0, 0), (0, SUBLANES - A_HEADS), (0, 0)))
    cb = jnp.transpose(cache_b_lat, (0, 1, 3, 2))
    cc = cache_c_kv.reshape(depth, n_phys, 2 * PAGE, C_V)
    caches = (ca, cl, cb, cc)

    outs_p, outs_s = [], []
    for l in range(depth):
        p = layers[l]
        qa, kva, kva16, lf, qb, kb, kb16, qc, kvc, kvc16 = _in_proj(
            xp, tabs_p, p, tm_in_p, lp // tm_in_p, sample=False)
        kbias = _fox_bias(lf, bsz, lp)
        mix = _prompt_attn(qa, qb, qc, kva16, kb16, kvc16, kbias, p, bsz, lp)
        xp = _out_mlp(xp, mix, p, tm_mlp_p, "out_mlp_prompt")
        outs_p.append((kva, lf, kb, kvc))
        qa, kva, lf, qb, kb, qc, kvc, cn = _in_proj(xs, tabs_s, p, tm_s, 1, sample=True, tri=tri)
        mix = _decode_attn(jnp.full((1,), l, jnp.int32), page_table, qa, qb, qc, kva, kb, kvc, cn, p,
                           caches, db, tn)
        xs = _out_mlp(xs, mix, p, tm_s, "out_mlp_sample")
        outs_s.append((kva, lf, kb, kvc))

    def assemble(outs, nb, t_all, t_keep):
        cut = lambda a, w: jnp.stack([o.reshape(nb, t_all, o.shape[-1])[:, :t_keep, :w] for o in a])
        akv = cut([o[0] for o in outs], 256).reshape(depth, nb, t_keep, 2, A_KV_HEADS, A_DIM)
        alf = jnp.stack([o[1].reshape(nb, t_all, LANES)[:, :t_keep, LOGF_LANE:LOGF_LANE + A_HEADS]
                         for o in outs])
        blat = cut([o[2] for o in outs], B_KV_LORA + B_ROPE)
        ckv = cut([o[3] for o in outs], 256).reshape(depth, nb, t_keep, 2, 1, C_V)
        return akv, alf, blat, ckv

    y_prompt = xp.reshape(bsz, lp, dm)[:, N_META:L]
    y_sample = xs.reshape(db, tn, dm)
    return (y_prompt, y_sample) + assemble(outs_p, bsz, lp, L) + assemble(outs_s, db, tn, tn)
```
